```python
import jax, jax.numpy as jnp
from jax import lax
import numpy as np

D_MODEL = 2048
BATCH = 4
SEQ = 4096
DEPTH = 2

GRID_W = 64
CTX_LEN = 256
N_MIXERS = 2
N_HEADS = 16
HEAD_DIM = D_MODEL // N_HEADS
WIN_H = 8
WIN_W = 16
CHUNK = 128
N_GROUPS = 16
GROUP_DIM = D_MODEL // N_GROUPS
D_FF = -(-(8 * D_MODEL) // (3 * 256)) * 256
N_NA = (DEPTH + 1) // 2
N_GM = DEPTH // 2
ALPHA = (2 * DEPTH) ** 0.25
BETA = (8 * DEPTH) ** -0.25
LN_EPS = 1e-5

kernel_name = "hybrid_natten_gmlp_deepnorm_dit"


def layer_norm(x, g, b):
    xf = x.astype(jnp.float32)
    mu = jnp.mean(xf, axis=-1, keepdims=True)
    var = jnp.mean(jnp.square(xf - mu), axis=-1, keepdims=True)
    return ((xf - mu) * lax.rsqrt(var + LN_EPS)).astype(x.dtype) * g + b


def ada_params(cond, w, b):
    return jnp.split(jax.nn.silu(cond) @ w + b, 6, axis=-1)


def modulate(h, shift, scale):
    return h * (1 + scale) + shift


def swiglu(h, w_in, w_out):
    gate, up = jnp.split(h @ w_in, 2, axis=-1)
    return (jax.nn.silu(gate) * up) @ w_out


def to_heads(t):
    b, n, _ = t.shape
    return jnp.transpose(t.reshape(b, n, N_HEADS, HEAD_DIM), (0, 2, 1, 3))


def neighbourhood_attention(q, k, v, ctx_k, ctx_v, rpb):
    bsz, nh, rows, width, hd = q.shape
    kh, kw = min(WIN_H, rows), min(WIN_W, width)
    col = np.arange(width)
    col_idx = np.clip(col - kw // 2, 0, width - kw)[:, None] + np.arange(kw)[None, :]
    col_rel = col_idx - col[:, None] + (WIN_W - 1)
    scale = hd ** -0.5

    def one_row(args):
        q_r, r = args
        r0 = jnp.clip(r - kh // 2, 0, rows - kh)
        k_win = jnp.take(lax.dynamic_slice_in_dim(k, r0, kh, axis=2), col_idx, axis=3)
        v_win = jnp.take(lax.dynamic_slice_in_dim(v, r0, kh, axis=2), col_idx, axis=3)
        row_rel = r0 + jnp.arange(kh) - r + (WIN_H - 1)
        bias = rpb[:, row_rel[:, None, None], col_rel[None, :, :]]
        s_win = (jnp.einsum("bhcd,bhicjd->bhcij", q_r, k_win) * scale
                 + jnp.transpose(bias, (0, 2, 1, 3)))
        s_ctx = jnp.einsum("bhcd,bhld->bhcl", q_r, ctx_k) * scale
        s = jnp.concatenate([s_win.reshape(bsz, nh, width, kh * kw), s_ctx], axis=-1)
        p = jax.nn.softmax(s.astype(jnp.float32), axis=-1).astype(v.dtype)
        p_win = p[..., :kh * kw].reshape(bsz, nh, width, kh, kw)
        return (jnp.einsum("bhcij,bhicjd->bhcd", p_win, v_win)
                + jnp.einsum("bhcl,bhld->bhcd", p[..., kh * kw:], ctx_v))

    out = lax.map(one_row, (jnp.moveaxis(q, 2, 0), jnp.arange(rows)))
    return jnp.moveaxis(out, 0, 2)


def na_mixer(a, ac, w_qkv, w_o, rpb, rows, ctx_queries):
    bsz, n, _ = a.shape
    q, k, v = jnp.split(a @ w_qkv, 3, axis=-1)

    def to_grid(t):
        return jnp.transpose(t.reshape(bsz, rows, GRID_W, N_HEADS, HEAD_DIM), (0, 3, 1, 2, 4))

    if ctx_queries:
        qc, kc, vc = jnp.split(ac @ w_qkv, 3, axis=-1)
    else:
        kc, vc = jnp.split(ac @ w_qkv[:, D_MODEL:], 2, axis=-1)
    kc_h, vc_h = to_heads(kc), to_heads(vc)
    o = neighbourhood_attention(to_grid(q), to_grid(k), to_grid(v), kc_h, vc_h, rpb)
    y = jnp.transpose(o, (0, 2, 3, 1, 4)).reshape(bsz, n, D_MODEL) @ w_o
    yc = None
    if ctx_queries:
        s = jnp.einsum("bhld,bhmd->bhlm", to_heads(qc), kc_h) * HEAD_DIM ** -0.5
        p = jax.nn.softmax(s.astype(jnp.float32), axis=-1).astype(vc_h.dtype)
        yc = jnp.einsum("bhlm,bhmd->blhd", p, vc_h).reshape(ac.shape) @ w_o
    return y, yc


def spatial_gating_mlp(a, w_in, ln_g, ln_b, w_s, b_s, w_out):
    bsz, n, _ = a.shape
    u, v = jnp.split(jax.nn.gelu(a @ w_in, approximate=False), 2, axis=-1)
    v = layer_norm(v, ln_g, ln_b).reshape(bsz, n // CHUNK, CHUNK, N_GROUPS, GROUP_DIM)
    v = jnp.einsum("gpq,bkqgc->bkpgc", w_s, v) + jnp.transpose(b_s)[:, :, None]
    return (u * v.reshape(bsz, n, D_MODEL)) @ w_out


def setup_inputs(seed: int = 0) -> dict:
    key = jax.random.key(seed)
    ks = jax.random.split(key, 19)
    nrm = jax.random.normal
    f32 = jnp.float32
    D = D_MODEL
    return {
        "x": nrm(ks[0], (BATCH, SEQ, D), f32),
        "c": nrm(ks[1], (BATCH, D), f32),
        "ctx": nrm(ks[2], (BATCH, CTX_LEN, D), f32),
        "c_ctx": nrm(ks[3], (D,), f32),
        "ada_w": nrm(ks[4], (DEPTH, D, 6 * D), f32) * D ** -0.5,
        "ada_b": nrm(ks[5], (DEPTH, 6 * D), f32) * 0.01,
        "ln_g": 1.0 + 0.02 * nrm(ks[6], (DEPTH, 2, D), f32),
        "ln_b": 0.02 * nrm(ks[7], (DEPTH, 2, D), f32),
        "na_w_qkv": nrm(ks[8], (N_NA, D, 3 * D), f32) * D ** -0.5,
        "na_w_o": nrm(ks[9], (N_NA, D, D), f32) * (D ** -0.5 * BETA),
        "na_rpb": 0.1 * nrm(ks[10], (N_NA, N_HEADS, 2 * WIN_H - 1, 2 * WIN_W - 1), f32),
        "gm_w_in": nrm(ks[11], (N_GM, D, 2 * D), f32) * D ** -0.5,
        "gm_ln_g": 1.0 + 0.02 * nrm(ks[12], (N_GM, D), f32),
        "gm_ln_b": 0.02 * nrm(ks[13], (N_GM, D), f32),
        "gm_w_s": nrm(ks[14], (N_GM, N_GROUPS, CHUNK, CHUNK), f32) * CHUNK ** -0.5,
        "gm_b_s": 0.02 * nrm(ks[15], (N_GM, N_GROUPS, CHUNK), f32),
        "gm_w_out": nrm(ks[16], (N_GM, D, D), f32) * (D ** -0.5 * BETA),
        "ffn_w_in": nrm(ks[17], (DEPTH, D, 2 * D_FF), f32) * D ** -0.5,
        "ffn_w_out": nrm(ks[18], (DEPTH, D_FF, D), f32) * (D_FF ** -0.5 * BETA),
    }


def reference(x, c, ctx, c_ctx, ada_w, ada_b, ln_g, ln_b, na_w_qkv, na_w_o, na_rpb,
              gm_w_in, gm_ln_g, gm_ln_b, gm_w_s, gm_b_s, gm_w_out, ffn_w_in, ffn_w_out):
    rows = x.shape[1] // GRID_W
    h, hc = x, ctx
    for i in range(DEPTH):
        kind = i % N_MIXERS
        j = i // N_MIXERS
        ctx_out = any(l % N_MIXERS == 0 for l in range(i + 1, DEPTH))
        sh1, sc1, g1, sh2, sc2, g2 = ada_params(c[:, None, :], ada_w[i], ada_b[i])
        a = modulate(h, sh1, sc1)
        if kind == 0 or ctx_out:
            csh1, csc1, cg1, csh2, csc2, cg2 = ada_params(c_ctx, ada_w[i], ada_b[i])
            ac = modulate(hc, csh1, csc1)
        if kind == 0:
            y, yc = na_mixer(a, ac, na_w_qkv[j], na_w_o[j], na_rpb[j], rows, ctx_out)
        else:
            gm = (gm_w_in[j], gm_ln_g[j], gm_ln_b[j], gm_w_s[j], gm_b_s[j], gm_w_out[j])
            y = spatial_gating_mlp(a, *gm)
            if ctx_out:
                yc = spatial_gating_mlp(ac, *gm)
        h = layer_norm(ALPHA * h + g1 * y, ln_g[i, 0], ln_b[i, 0])
        h = layer_norm(ALPHA * h + g2 * swiglu(modulate(h, sh2, sc2), ffn_w_in[i], ffn_w_out[i]),
                       ln_g[i, 1], ln_b[i, 1])
        if ctx_out:
            hc = layer_norm(ALPHA * hc + cg1 * yc, ln_g[i, 0], ln_b[i, 0])
            hc = layer_norm(ALPHA * hc + cg2 * swiglu(modulate(hc, csh2, csc2), ffn_w_in[i], ffn_w_out[i]),
                            ln_g[i, 1], ln_b[i, 1])
    return h
```

```python
import functools

import numpy as np
import jax
import jax.numpy as jnp
from jax import lax
from jax.experimental import pallas as pl
from jax.experimental.pallas import tpu as pltpu

D_MODEL = 2048
GRID_W = 64
N_HEADS = 16
HEAD_DIM = D_MODEL // N_HEADS
WIN_H = 8
WIN_W = 16
CHUNK = 128
N_GROUPS = 16
GROUP_DIM = D_MODEL // N_GROUPS
DEPTH = 2
ALPHA = (2 * DEPTH) ** 0.25
LN_EPS = 1e-5

VMEM_LIMIT_BYTES = 56 * 1024 * 1024
MASK_VALUE = -1e30

Q_ROWS = 2
KEY_ROWS = Q_ROWS + WIN_H - 1
Q_BLK = Q_ROWS * GRID_W
K_BLK = KEY_ROWS * GRID_W

BF16 = jnp.bfloat16
F32 = jnp.float32


def _params(*sem):
    return pltpu.CompilerParams(dimension_semantics=sem, vmem_limit_bytes=VMEM_LIMIT_BYTES)


def _layer_norm(x, g, b):
    mu = jnp.mean(x, axis=-1, keepdims=True)
    xc = x - mu
    var = jnp.mean(xc * xc, axis=-1, keepdims=True)
    return xc * lax.rsqrt(var + LN_EPS) * g + b


def _silu(x):
    return x * (1.0 / (1.0 + jnp.exp(-x)))


def _dot(a, b):
    return jnp.dot(a, b, preferred_element_type=F32)


def _ada_kernel(cond_ref, w_ref, b_ref, o_ref):
    s = _silu(cond_ref[...]).astype(BF16)
    o_ref[0] = _dot(s, w_ref[0].astype(BF16)) + b_ref[0]


def _ada_params(cond, ada_w, ada_b, tn=1024):
    depth, d, n6 = ada_w.shape
    rows = cond.shape[0]
    return pl.pallas_call(
        _ada_kernel,
        grid=(depth, n6 // tn),
        in_specs=[
            pl.BlockSpec((rows, d), lambda l, j: (0, 0)),
            pl.BlockSpec((1, d, tn), lambda l, j: (l, 0, j)),
            pl.BlockSpec((1, 1, tn), lambda l, j: (l, 0, j)),
        ],
        out_specs=pl.BlockSpec((1, rows, tn), lambda l, j: (l, 0, j)),
        out_shape=jax.ShapeDtypeStruct((depth, rows, n6), F32),
        compiler_params=_params("parallel", "parallel"),
        name="ada_params",
    )(cond, ada_w, ada_b.reshape(depth, 1, n6))


def _mod_matmul_kernel(x_ref, sh_ref, sc_ref, w_ref, o_ref, a_ref):
    @pl.when(pl.program_id(1) == 0)
    def _():
        a_ref[...] = (x_ref[...] * (1.0 + sc_ref[0]) + sh_ref[0]).astype(BF16)

    o_ref[...] = _dot(a_ref[...], w_ref[...]).astype(o_ref.dtype)


def _mod_matmul(x, sh, sc, w, rows_per_batch, tm, tn, col_start=0):
    m, d = x.shape
    n = w.shape[1] - col_start
    col_blk = col_start // tn
    tiles_per_batch = rows_per_batch // tm
    if sh.shape[0] == 1:
        mod_idx = lambda i, j: (0, 0, 0)
    else:
        mod_idx = lambda i, j: (i // tiles_per_batch, 0, 0)
    return pl.pallas_call(
        _mod_matmul_kernel,
        grid=(m // tm, n // tn),
        in_specs=[
            pl.BlockSpec((tm, d), lambda i, j: (i, 0)),
            pl.BlockSpec((1, 1, d), mod_idx),
            pl.BlockSpec((1, 1, d), mod_idx),
            pl.BlockSpec((d, tn), lambda i, j: (0, col_blk + j)),
        ],
        out_specs=pl.BlockSpec((tm, tn), lambda i, j: (i, j)),
        out_shape=jax.ShapeDtypeStruct((m, n), BF16),
        scratch_shapes=[pltpu.VMEM((tm, d), BF16)],
        compiler_params=_params("parallel", "arbitrary"),
        name="mod_matmul",
    )(x, sh, sc, w)


def _attn_bias_tables(rpb, rows):
    n_blocks = rows // Q_ROWS
    last_kb = rows - KEY_ROWS
    geoms = [(0, 0), (Q_ROWS, 0), (2 * Q_ROWS, 0),
             ((n_blocks - 2) * Q_ROWS, last_kb), ((n_blocks - 1) * Q_ROWS, last_kb)]
    dr = np.arange(Q_ROWS)[:, None, None, None]
    c = np.arange(GRID_W)[None, :, None, None]
    ki = np.arange(KEY_ROWS)[None, None, :, None]
    kc = np.arange(GRID_W)[None, None, None, :]
    row_idx, col_idx, valid = [], [], []
    for rb, kb in geoms:
        r = rb + dr
        kr = kb + ki
        r0 = np.clip(r - WIN_H // 2, 0, rows - WIN_H)
        c0 = np.clip(c - WIN_W // 2, 0, GRID_W - WIN_W)
        ok = (kr >= r0) & (kr < r0 + WIN_H) & (kc >= c0) & (kc < c0 + WIN_W)
        shape = (Q_ROWS, GRID_W, KEY_ROWS, GRID_W)
        row_idx.append(np.broadcast_to(np.clip(kr - r + WIN_H - 1, 0, 2 * WIN_H - 2), shape).reshape(Q_BLK, K_BLK))
        col_idx.append(np.broadcast_to(np.clip(kc - c + WIN_W - 1, 0, 2 * WIN_W - 2), shape).reshape(Q_BLK, K_BLK))
        valid.append(np.broadcast_to(ok, shape).reshape(Q_BLK, K_BLK))
    row_idx, col_idx, valid = np.stack(row_idx), np.stack(col_idx), np.stack(valid)
    table = rpb[:, row_idx, col_idx]
    table = jnp.where(valid[None], table, MASK_VALUE)
    return jnp.transpose(table, (1, 0, 2, 3))


def _attn_kernel(q_ref, k_ref, v_ref, kc_ref, vc_ref, bias_ref, o_ref, *, n_blocks, last_kb):
    scale = HEAD_DIM ** -0.5
    kc = kc_ref[0]
    vc = vc_ref[0]
    contract_last = (((1,), (1,)), ((), ()))

    def body(t, carry):
        kb = jnp.clip(Q_ROWS * t - WIN_H // 2, 0, last_kb)
        geom = jnp.where(t < 2, t, jnp.where(t >= n_blocks - 2, t - (n_blocks - 5), 2))
        q0 = pl.multiple_of(t * Q_BLK, Q_BLK)
        k0 = pl.multiple_of(kb * GRID_W, GRID_W)
        q = q_ref[0, pl.ds(q0, Q_BLK), :]
        k = k_ref[0, pl.ds(k0, K_BLK), :]
        v = v_ref[0, pl.ds(k0, K_BLK), :]
        s = lax.dot_general(q, k, contract_last, preferred_element_type=F32) * scale + bias_ref[geom, 0]
        sc = lax.dot_general(q, kc, contract_last, preferred_element_type=F32) * scale
        m = jnp.maximum(jnp.max(s, axis=-1, keepdims=True), jnp.max(sc, axis=-1, keepdims=True))
        p = jnp.exp(s - m)
        pc = jnp.exp(sc - m)
        denom = jnp.sum(p, axis=-1, keepdims=True) + jnp.sum(pc, axis=-1, keepdims=True)
        o = _dot(p.astype(BF16), v) + _dot(pc.astype(BF16), vc)
        o_ref[0, pl.ds(q0, Q_BLK), :] = (o / denom).astype(o_ref.dtype)
        return carry

    lax.fori_loop(0, n_blocks, body, 0)


def _neighbourhood_attention(qkv, kvc, bias):
    bsz, n, _ = qkv.shape
    ctx_len = kvc.shape[1]
    rows = n // GRID_W
    n_blocks = rows // Q_ROWS
    kern = functools.partial(_attn_kernel, n_blocks=n_blocks, last_kb=rows - KEY_ROWS)
    seq_spec = lambda off: pl.BlockSpec((1, n, HEAD_DIM), lambda h, b: (b, 0, off + h))
    ctx_spec = lambda off: pl.BlockSpec((1, ctx_len, HEAD_DIM), lambda h, b: (b, 0, off + h))
    return pl.pallas_call(
        kern,
        grid=(N_HEADS, bsz),
        in_specs=[
            seq_spec(0), seq_spec(N_HEADS), seq_spec(2 * N_HEADS),
            ctx_spec(0), ctx_spec(N_HEADS),
            pl.BlockSpec((bias.shape[0], 1, Q_BLK, K_BLK), lambda h, b: (0, h, 0, 0)),
        ],
        out_specs=pl.BlockSpec((1, n, HEAD_DIM), lambda h, b: (b, 0, h)),
        out_shape=jax.ShapeDtypeStruct((bsz, n, D_MODEL), BF16),
        compiler_params=_params("parallel", "parallel"),
        name="neighbourhood_attention",
    )(qkv, qkv, qkv, kvc, kvc, bias)


def _proj_res_ln_kernel(y_ref, w_ref, h_ref, g_ref, lg_ref, lb_ref, o_ref):
    y = _dot(y_ref[...], w_ref[...])
    o_ref[...] = _layer_norm(ALPHA * h_ref[...] + g_ref[0] * y, lg_ref[...], lb_ref[...])


def _proj_res_ln(y, w, h, gate, ln_g, ln_b, rows_per_batch, tm=512):
    m, k = y.shape
    d = w.shape[1]
    tiles_per_batch = rows_per_batch // tm
    return pl.pallas_call(
        _proj_res_ln_kernel,
        grid=(m // tm,),
        in_specs=[
            pl.BlockSpec((tm, k), lambda i: (i, 0)),
            pl.BlockSpec((k, d), lambda i: (0, 0)),
            pl.BlockSpec((tm, d), lambda i: (i, 0)),
            pl.BlockSpec((1, 1, d), lambda i: (i // tiles_per_batch, 0, 0)),
            pl.BlockSpec((1, d), lambda i: (0, 0)),
            pl.BlockSpec((1, d), lambda i: (0, 0)),
        ],
        out_specs=pl.BlockSpec((tm, d), lambda i: (i, 0)),
        out_shape=jax.ShapeDtypeStruct((m, d), F32),
        compiler_params=_params("parallel"),
        name="proj_res_ln",
    )(y, w, h, gate, ln_g.reshape(1, d), ln_b.reshape(1, d))


def _ffn_kernel(h_ref, sh_ref, sc_ref, g_ref, wg_ref, wu_ref, wo_ref, lg_ref, lb_ref, o_ref, a_ref, acc_ref):
    f = pl.program_id(1)

    @pl.when(f == 0)
    def _():
        a_ref[...] = (h_ref[...] * (1.0 + sc_ref[0]) + sh_ref[0]).astype(BF16)
        acc_ref[...] = jnp.zeros_like(acc_ref)

    a = a_ref[...]
    act = _silu(_dot(a, wg_ref[...])) * _dot(a, wu_ref[...])
    acc_ref[...] += _dot(act.astype(BF16), wo_ref[...])

    @pl.when(f == pl.num_programs(1) - 1)
    def _():
        o_ref[...] = _layer_norm(ALPHA * h_ref[...] + g_ref[0] * acc_ref[...], lg_ref[...], lb_ref[...])


def _ffn(h, sh, sc, gate, w_in, w_out, ln_g, ln_b, rows_per_batch, tm=512, tf=512):
    m, d = h.shape
    d_ff = w_out.shape[0]
    n_f = d_ff // tf
    tiles_per_batch = rows_per_batch // tm
    mod_spec = pl.BlockSpec((1, 1, d), lambda i, f: (i // tiles_per_batch, 0, 0))
    vec_spec = pl.BlockSpec((1, d), lambda i, f: (0, 0))
    return pl.pallas_call(
        _ffn_kernel,
        grid=(m // tm, n_f),
        in_specs=[
            pl.BlockSpec((tm, d), lambda i, f: (i, 0)),
            mod_spec, mod_spec, mod_spec,
            pl.BlockSpec((d, tf), lambda i, f: (0, f)),
            pl.BlockSpec((d, tf), lambda i, f: (0, n_f + f)),
            pl.BlockSpec((tf, d), lambda i, f: (f, 0)),
            vec_spec, vec_spec,
        ],
        out_specs=pl.BlockSpec((tm, d), lambda i, f: (i, 0)),
        out_shape=jax.ShapeDtypeStruct((m, d), F32),
        scratch_shapes=[pltpu.VMEM((tm, d), BF16), pltpu.VMEM((tm, d), F32)],
        compiler_params=_params("parallel", "arbitrary"),
        name="swiglu_ffn",
    )(h, sh, sc, gate, w_in, w_in, w_out, ln_g.reshape(1, d), ln_b.reshape(1, d))


def _gmlp_kernel(h_ref, sh_ref, sc_ref, wu_ref, wv_ref, lg_ref, lb_ref, ws_ref, bs_ref, o_ref, u_ref, v_ref):
    a = (h_ref[...] * (1.0 + sc_ref[0]) + sh_ref[0]).astype(BF16)

    def gelu(z):
        return 0.5 * z * (1.0 + lax.erf(z * (2.0 ** -0.5)))

    u_ref[...] = gelu(_dot(a, wu_ref[...]))
    v_ref[...] = _layer_norm(gelu(_dot(a, wv_ref[...])), lg_ref[...], lb_ref[...]).astype(BF16)
    for ci in range(h_ref.shape[0] // CHUNK):
        rows = slice(ci * CHUNK, (ci + 1) * CHUNK)
        for g in range(N_GROUPS):
            cols = slice(g * GROUP_DIM, (g + 1) * GROUP_DIM)
            mixed = _dot(ws_ref[g], v_ref[rows, cols]) + bs_ref[:, cols]
            o_ref[rows, cols] = (u_ref[rows, cols] * mixed).astype(o_ref.dtype)


def _gmlp_gate(h, sh, sc, w_in, ln_g, ln_b, w_s, b_s_full, rows_per_batch, tm=256):
    m, d = h.shape
    tiles_per_batch = rows_per_batch // tm
    mod_spec = pl.BlockSpec((1, 1, d), lambda i: (i // tiles_per_batch, 0, 0))
    vec_spec = pl.BlockSpec((1, d), lambda i: (0, 0))
    return pl.pallas_call(
        _gmlp_kernel,
        grid=(m // tm,),
        in_specs=[
            pl.BlockSpec((tm, d), lambda i: (i, 0)),
            mod_spec, mod_spec,
            pl.BlockSpec((d, d), lambda i: (0, 0)),
            pl.BlockSpec((d, d), lambda i: (0, 1)),
            vec_spec, vec_spec,
            pl.BlockSpec((N_GROUPS, CHUNK, CHUNK), lambda i: (0, 0, 0)),
            pl.BlockSpec((CHUNK, d), lambda i: (0, 0)),
        ],
        out_specs=pl.BlockSpec((tm, d), lambda i: (i, 0)),
        out_shape=jax.ShapeDtypeStruct((m, d), BF16),
        scratch_shapes=[pltpu.VMEM((tm, d), F32), pltpu.VMEM((tm, d), BF16)],
        compiler_params=_params("parallel"),
        name="gmlp_gate",
    )(h, sh, sc, w_in, w_in, ln_g.reshape(1, d), ln_b.reshape(1, d), w_s, b_s_full)


def kernel(x, c, ctx, c_ctx, ada_w, ada_b, ln_g, ln_b, na_w_qkv, na_w_o, na_rpb, gm_w_in, gm_ln_g, gm_ln_b,
           gm_w_s, gm_b_s, gm_w_out, ffn_w_in, ffn_w_out):
    bsz, n, d = x.shape
    ctx_len = ctx.shape[1]
    rows = n // GRID_W
    assert d == D_MODEL and n % (GRID_W * Q_ROWS) == 0 and rows >= KEY_ROWS + 2 * Q_ROWS

    cond = jnp.concatenate([c, c_ctx[None, :], jnp.zeros((8 - bsz - 1, d), F32)], axis=0)
    ada = _ada_params(cond, ada_w, ada_b)

    def mod_vectors(layer):
        parts = jnp.split(ada[layer], 6, axis=-1)
        latent = [p[:bsz].reshape(bsz, 1, d) for p in parts]
        context = [p[bsz:bsz + 1].reshape(1, 1, d) for p in parts]
        return latent, context

    h = x.reshape(bsz * n, d)

    (sh1, sc1, g1, sh2, sc2, g2), (csh1, csc1, _, _, _, _) = mod_vectors(0)
    w_qkv = na_w_qkv[0].astype(BF16)
    qkv = _mod_matmul(h, sh1, sc1, w_qkv, n, tm=1024, tn=1024)
    kvc = _mod_matmul(ctx.reshape(bsz * ctx_len, d), csh1, csc1, w_qkv, ctx_len, tm=ctx_len, tn=1024, col_start=d)
    bias = _attn_bias_tables(na_rpb[0], rows)
    o = _neighbourhood_attention(qkv.reshape(bsz, n, 3 * d), kvc.reshape(bsz, ctx_len, 2 * d), bias)
    h = _proj_res_ln(o.reshape(bsz * n, d), na_w_o[0].astype(BF16), h, g1, ln_g[0, 0], ln_b[0, 0], n)
    h = _ffn(h, sh2, sc2, g2, ffn_w_in[0].astype(BF16), ffn_w_out[0].astype(BF16), ln_g[0, 1], ln_b[0, 1], n)

    (sh1, sc1, g1, sh2, sc2, g2), _ = mod_vectors(1)
    b_s_full = jnp.repeat(jnp.transpose(gm_b_s[0]), GROUP_DIM, axis=1)
    uv = _gmlp_gate(h, sh1, sc1, gm_w_in[0].astype(BF16), gm_ln_g[0], gm_ln_b[0], gm_w_s[0].astype(BF16),
                    b_s_full, n)
    h = _proj_res_ln(uv, gm_w_out[0].astype(BF16), h, g1, ln_g[1, 0], ln_b[1, 0], n)
    h = _ffn(h, sh2, sc2, g2, ffn_w_in[1].astype(BF16), ffn_w_out[1].astype(BF16), ln_g[1, 1], ln_b[1, 1], n)
    return h.reshape(bsz, n, d)
```

```python
import functools

import numpy as np
import jax
import jax.numpy as jnp
from jax import lax
from jax.experimental import pallas as pl
from jax.experimental.pallas import tpu as pltpu

D_MODEL = 2048
GRID_W = 64
N_HEADS = 16
HEAD_DIM = D_MODEL // N_HEADS
WIN_H = 8
WIN_W = 16
CHUNK = 128
N_GROUPS = 16
GROUP_DIM = D_MODEL // N_GROUPS
DEPTH = 2
ALPHA = (2 * DEPTH) ** 0.25
LN_EPS = 1e-5

VMEM_LIMIT_BYTES = 56 * 1024 * 1024
MASK_VALUE = -1e30

Q_ROWS = 2
KEY_ROWS = Q_ROWS + WIN_H - 1
Q_BLK = Q_ROWS * GRID_W
K_BLK = KEY_ROWS * GRID_W
ATTN_UNROLL = 4

BF16 = jnp.bfloat16
F32 = jnp.float32


def _params(*sem):
    return pltpu.CompilerParams(dimension_semantics=sem, vmem_limit_bytes=VMEM_LIMIT_BYTES)


def _layer_norm(x, g, b):
    mu = jnp.mean(x, axis=-1, keepdims=True)
    xc = x - mu
    var = jnp.mean(xc * xc, axis=-1, keepdims=True)
    return xc * lax.rsqrt(var + LN_EPS) * g + b


def _silu(x):
    return x * (1.0 / (1.0 + jnp.exp(-x)))


def _dot(a, b):
    return jnp.dot(a, b, preferred_element_type=F32)


def _ada_kernel(cond_ref, w_ref, b_ref, o_ref):
    s = _silu(cond_ref[...]).astype(BF16)
    o_ref[0] = _dot(s, w_ref[0].astype(BF16)) + b_ref[0]


def _ada_params(cond, ada_w, ada_b, tn=1024):
    depth, d, n6 = ada_w.shape
    rows = cond.shape[0]
    return pl.pallas_call(
        _ada_kernel,
        grid=(depth, n6 // tn),
        in_specs=[
            pl.BlockSpec((rows, d), lambda l, j: (0, 0)),
            pl.BlockSpec((1, d, tn), lambda l, j: (l, 0, j)),
            pl.BlockSpec((1, 1, tn), lambda l, j: (l, 0, j)),
        ],
        out_specs=pl.BlockSpec((1, rows, tn), lambda l, j: (l, 0, j)),
        out_shape=jax.ShapeDtypeStruct((depth, rows, n6), F32),
        compiler_params=_params("parallel", "parallel"),
        name="ada_params",
    )(cond, ada_w, ada_b.reshape(depth, 1, n6))


def _mod_matmul_kernel(x_ref, sh_ref, sc_ref, w_ref, o_ref, a_ref, *, scaled_blocks, out_scale):
    @pl.when(pl.program_id(1) == 0)
    def _():
        a_ref[...] = (x_ref[...] * (1.0 + sc_ref[0]) + sh_ref[0]).astype(BF16)

    y = _dot(a_ref[...], w_ref[...])
    if scaled_blocks:
        y = y * jnp.where(pl.program_id(1) < scaled_blocks, out_scale, 1.0)
    o_ref[...] = y.astype(o_ref.dtype)


def _mod_matmul(x, sh, sc, w, rows_per_batch, tm, tn, col_start=0, scaled_cols=0, out_scale=1.0, name="mod_matmul"):
    m, d = x.shape
    n = w.shape[1] - col_start
    col_blk = col_start // tn
    tiles_per_batch = rows_per_batch // tm
    kern = functools.partial(_mod_matmul_kernel, scaled_blocks=scaled_cols // tn, out_scale=out_scale)
    if sh.shape[0] == 1:
        mod_idx = lambda i, j: (0, 0, 0)
    else:
        mod_idx = lambda i, j: (i // tiles_per_batch, 0, 0)
    return pl.pallas_call(
        kern,
        grid=(m // tm, n // tn),
        in_specs=[
            pl.BlockSpec((tm, d), lambda i, j: (i, 0)),
            pl.BlockSpec((1, 1, d), mod_idx),
            pl.BlockSpec((1, 1, d), mod_idx),
            pl.BlockSpec((d, tn), lambda i, j: (0, col_blk + j)),
        ],
        out_specs=pl.BlockSpec((tm, tn), lambda i, j: (i, j)),
        out_shape=jax.ShapeDtypeStruct((m, n), BF16),
        scratch_shapes=[pltpu.VMEM((tm, d), BF16)],
        compiler_params=_params("parallel", "arbitrary"),
        name=name,
    )(x, sh, sc, w)


def _attn_bias_tables(rpb, rows):
    n_blocks = rows // Q_ROWS
    last_kb = rows - KEY_ROWS
    geoms = [(0, 0), (Q_ROWS, 0), (2 * Q_ROWS, 0),
             ((n_blocks - 2) * Q_ROWS, last_kb), ((n_blocks - 1) * Q_ROWS, last_kb)]
    n_heads, n_row_rel, _ = rpb.shape
    period = 2 * GRID_W + 1
    pad = jnp.zeros((n_heads, n_row_rel, period - (2 * WIN_W - 1)), rpb.dtype)
    base = jnp.concatenate([rpb[..., WIN_W - 1:], pad, rpb[..., :WIN_W - 1]], axis=-1)
    toep = jnp.tile(base, (1, 1, GRID_W))[..., :GRID_W * (period - 1)]
    toep = toep.reshape(n_heads, n_row_rel, GRID_W, period - 1)[..., :GRID_W]

    c = np.arange(GRID_W)[:, None]
    kc = np.arange(GRID_W)[None, :]
    c0 = np.clip(c - WIN_W // 2, 0, GRID_W - WIN_W)
    col_ok = (kc >= c0) & (kc < c0 + WIN_W)
    masked = jnp.full((n_heads, GRID_W, GRID_W), MASK_VALUE, rpb.dtype)
    tiles = jnp.where(col_ok[None, None], toep, MASK_VALUE)
    tables = []
    for rb, kb in geoms:
        q_rows = []
        for dr in range(Q_ROWS):
            r = rb + dr
            r0 = min(max(r - WIN_H // 2, 0), rows - WIN_H)
            blocks = [tiles[:, kb + i - r + WIN_H - 1] if r0 <= kb + i < r0 + WIN_H else masked
                      for i in range(KEY_ROWS)]
            q_rows.append(jnp.concatenate(blocks, axis=-1))
        tables.append(jnp.concatenate(q_rows, axis=1))
    return jnp.stack(tables)


def _attn_kernel(q_ref, k_ref, v_ref, kc_ref, vc_ref, bias_ref, o_ref, *, n_blocks, last_kb):
    kc = kc_ref[0]
    vc = vc_ref[0]
    contract_last = (((1,), (1,)), ((), ()))

    def body(t, carry):
        kb = jnp.clip(Q_ROWS * t - WIN_H // 2, 0, last_kb)
        geom = jnp.where(t < 2, t, jnp.where(t >= n_blocks - 2, t - (n_blocks - 5), 2))
        q0 = pl.multiple_of(t * Q_BLK, Q_BLK)
        k0 = pl.multiple_of(kb * GRID_W, GRID_W)
        q = q_ref[0, pl.ds(q0, Q_BLK), :]
        k = k_ref[0, pl.ds(k0, K_BLK), :]
        v = v_ref[0, pl.ds(k0, K_BLK), :]
        s = lax.dot_general(q, k, contract_last, preferred_element_type=F32) + bias_ref[geom, 0]
        sc = lax.dot_general(q, kc, contract_last, preferred_element_type=F32)
        m = jnp.maximum(jnp.max(s, axis=-1, keepdims=True), jnp.max(sc, axis=-1, keepdims=True))
        p = jnp.exp(s - m)
        pc = jnp.exp(sc - m)
        denom = jnp.sum(p, axis=-1, keepdims=True) + jnp.sum(pc, axis=-1, keepdims=True)
        o = _dot(p.astype(BF16), v) + _dot(pc.astype(BF16), vc)
        o_ref[0, pl.ds(q0, Q_BLK), :] = (o / denom).astype(o_ref.dtype)
        return carry

    lax.fori_loop(0, n_blocks, body, 0, unroll=ATTN_UNROLL)


def _neighbourhood_attention(qkv, kvc, bias):
    bsz, n, _ = qkv.shape
    ctx_len = kvc.shape[1]
    rows = n // GRID_W
    n_blocks = rows // Q_ROWS
    kern = functools.partial(_attn_kernel, n_blocks=n_blocks, last_kb=rows - KEY_ROWS)
    seq_spec = lambda off: pl.BlockSpec((1, n, HEAD_DIM), lambda h, b: (b, 0, off + h))
    ctx_spec = lambda off: pl.BlockSpec((1, ctx_len, HEAD_DIM), lambda h, b: (b, 0, off + h))
    return pl.pallas_call(
        kern,
        grid=(N_HEADS, bsz),
        in_specs=[
            seq_spec(0), seq_spec(N_HEADS), seq_spec(2 * N_HEADS),
            ctx_spec(0), ctx_spec(N_HEADS),
            pl.BlockSpec((bias.shape[0], 1, Q_BLK, K_BLK), lambda h, b: (0, h, 0, 0)),
        ],
        out_specs=pl.BlockSpec((1, n, HEAD_DIM), lambda h, b: (b, 0, h)),
        out_shape=jax.ShapeDtypeStruct((bsz, n, D_MODEL), BF16),
        compiler_params=_params("parallel", "parallel"),
        name="neighbourhood_attention",
    )(qkv, qkv, qkv, kvc, kvc, bias)


def _proj_res_ln_kernel(y_ref, w_ref, h_ref, g_ref, lg_ref, lb_ref, o_ref):
    y = _dot(y_ref[...], w_ref[...])
    o_ref[...] = _layer_norm(ALPHA * h_ref[...] + g_ref[0] * y, lg_ref[...], lb_ref[...])


def _proj_res_ln(y, w, h, gate, ln_g, ln_b, rows_per_batch, tm=512):
    m, k = y.shape
    d = w.shape[1]
    tiles_per_batch = rows_per_batch // tm
    return pl.pallas_call(
        _proj_res_ln_kernel,
        grid=(m // tm,),
        in_specs=[
            pl.BlockSpec((tm, k), lambda i: (i, 0)),
            pl.BlockSpec((k, d), lambda i: (0, 0)),
            pl.BlockSpec((tm, d), lambda i: (i, 0)),
            pl.BlockSpec((1, 1, d), lambda i: (i // tiles_per_batch, 0, 0)),
            pl.BlockSpec((1, d), lambda i: (0, 0)),
            pl.BlockSpec((1, d), lambda i: (0, 0)),
        ],
        out_specs=pl.BlockSpec((tm, d), lambda i: (i, 0)),
        out_shape=jax.ShapeDtypeStruct((m, d), F32),
        compiler_params=_params("parallel"),
        name="proj_res_ln",
    )(y, w, h, gate, ln_g.reshape(1, d), ln_b.reshape(1, d))


def _ffn_kernel(h_ref, sh_ref, sc_ref, g_ref, wg_ref, wu_ref, wo_ref, lg_ref, lb_ref, o_ref, a_ref, acc_ref):
    f = pl.program_id(1)

    @pl.when(f == 0)
    def _():
        a_ref[...] = (h_ref[...] * (1.0 + sc_ref[0]) + sh_ref[0]).astype(BF16)
        acc_ref[...] = jnp.zeros_like(acc_ref)

    a = a_ref[...]
    act = _silu(_dot(a, wg_ref[...])) * _dot(a, wu_ref[...])
    acc_ref[...] += _dot(act.astype(BF16), wo_ref[...])

    @pl.when(f == pl.num_programs(1) - 1)
    def _():
        o_ref[...] = _layer_norm(ALPHA * h_ref[...] + g_ref[0] * acc_ref[...], lg_ref[...], lb_ref[...])


def _ffn(h, sh, sc, gate, w_in, w_out, ln_g, ln_b, rows_per_batch, tm=512, tf=512):
    m, d = h.shape
    d_ff = w_out.shape[0]
    n_f = d_ff // tf
    tiles_per_batch = rows_per_batch // tm
    mod_spec = pl.BlockSpec((1, 1, d), lambda i, f: (i // tiles_per_batch, 0, 0))
    vec_spec = pl.BlockSpec((1, d), lambda i, f: (0, 0))
    return pl.pallas_call(
        _ffn_kernel,
        grid=(m // tm, n_f),
        in_specs=[
            pl.BlockSpec((tm, d), lambda i, f: (i, 0)),
            mod_spec, mod_spec, mod_spec,
            pl.BlockSpec((d, tf), lambda i, f: (0, f)),
            pl.BlockSpec((d, tf), lambda i, f: (0, n_f + f)),
            pl.BlockSpec((tf, d), lambda i, f: (f, 0)),
            vec_spec, vec_spec,
        ],
        out_specs=pl.BlockSpec((tm, d), lambda i, f: (i, 0)),
        out_shape=jax.ShapeDtypeStruct((m, d), F32),
        scratch_shapes=[pltpu.VMEM((tm, d), BF16), pltpu.VMEM((tm, d), F32)],
        compiler_params=_params("parallel", "arbitrary"),
        name="swiglu_ffn",
    )(h, sh, sc, gate, w_in, w_in, w_out, ln_g.reshape(1, d), ln_b.reshape(1, d))


def _gmlp_kernel(h_ref, sh_ref, sc_ref, wu_ref, wv_ref, lg_ref, lb_ref, ws_ref, bs_ref, o_ref, u_ref, v_ref):
    a = (h_ref[...] * (1.0 + sc_ref[0]) + sh_ref[0]).astype(BF16)

    def gelu(z):
        return 0.5 * z * (1.0 + lax.erf(z * (2.0 ** -0.5)))

    u_ref[...] = gelu(_dot(a, wu_ref[...]))
    v_ref[...] = _layer_norm(gelu(_dot(a, wv_ref[...])), lg_ref[...], lb_ref[...]).astype(BF16)
    for ci in range(h_ref.shape[0] // CHUNK):
        rows = slice(ci * CHUNK, (ci + 1) * CHUNK)
        for g in range(N_GROUPS):
            cols = slice(g * GROUP_DIM, (g + 1) * GROUP_DIM)
            mixed = _dot(ws_ref[g], v_ref[rows, cols]) + bs_ref[:, cols]
            o_ref[rows, cols] = (u_ref[rows, cols] * mixed).astype(o_ref.dtype)


def _gmlp_gate(h, sh, sc, w_in, ln_g, ln_b, w_s, b_s_full, rows_per_batch, tm=256):
    m, d = h.shape
    tiles_per_batch = rows_per_batch // tm
    mod_spec = pl.BlockSpec((1, 1, d), lambda i: (i // tiles_per_batch, 0, 0))
    vec_spec = pl.BlockSpec((1, d), lambda i: (0, 0))
    return pl.pallas_call(
        _gmlp_kernel,
        grid=(m // tm,),
        in_specs=[
            pl.BlockSpec((tm, d), lambda i: (i, 0)),
            mod_spec, mod_spec,
            pl.BlockSpec((d, d), lambda i: (0, 0)),
            pl.BlockSpec((d, d), lambda i: (0, 1)),
            vec_spec, vec_spec,
            pl.BlockSpec((N_GROUPS, CHUNK, CHUNK), lambda i: (0, 0, 0)),
            pl.BlockSpec((CHUNK, d), lambda i: (0, 0)),
        ],
        out_specs=pl.BlockSpec((tm, d), lambda i: (i, 0)),
        out_shape=jax.ShapeDtypeStruct((m, d), BF16),
        scratch_shapes=[pltpu.VMEM((tm, d), F32), pltpu.VMEM((tm, d), BF16)],
        compiler_params=_params("parallel"),
        name="gmlp_gate",
    )(h, sh, sc, w_in, w_in, ln_g.reshape(1, d), ln_b.reshape(1, d), w_s, b_s_full)


def kernel(x, c, ctx, c_ctx, ada_w, ada_b, ln_g, ln_b, na_w_qkv, na_w_o, na_rpb, gm_w_in, gm_ln_g, gm_ln_b,
           gm_w_s, gm_b_s, gm_w_out, ffn_w_in, ffn_w_out):
    bsz, n, d = x.shape
    ctx_len = ctx.shape[1]
    rows = n // GRID_W
    assert d == D_MODEL and n % (GRID_W * Q_ROWS) == 0 and rows >= KEY_ROWS + 2 * Q_ROWS

    cond = jnp.concatenate([c, c_ctx[None, :], jnp.zeros((8 - bsz - 1, d), F32)], axis=0)
    ada = _ada_params(cond, ada_w, ada_b)

    def mod_vectors(layer):
        parts = jnp.split(ada[layer], 6, axis=-1)
        latent = [p[:bsz].reshape(bsz, 1, d) for p in parts]
        context = [p[bsz:bsz + 1].reshape(1, 1, d) for p in parts]
        return latent, context

    h = x.reshape(bsz * n, d)

    (sh1, sc1, g1, sh2, sc2, g2), (csh1, csc1, _, _, _, _) = mod_vectors(0)
    w_qkv = na_w_qkv[0].astype(BF16)
    qkv = _mod_matmul(h, sh1, sc1, w_qkv, n, tm=1024, tn=1024, scaled_cols=d, out_scale=HEAD_DIM ** -0.5,
                      name="qkv_proj")
    kvc = _mod_matmul(ctx.reshape(bsz * ctx_len, d), csh1, csc1, w_qkv, ctx_len, tm=ctx_len, tn=1024, col_start=d,
                      name="ctx_kv_proj")
    bias = _attn_bias_tables(na_rpb[0], rows)
    o = _neighbourhood_attention(qkv.reshape(bsz, n, 3 * d), kvc.reshape(bsz, ctx_len, 2 * d), bias)
    h = _proj_res_ln(o.reshape(bsz * n, d), na_w_o[0].astype(BF16), h, g1, ln_g[0, 0], ln_b[0, 0], n)
    h = _ffn(h, sh2, sc2, g2, ffn_w_in[0].astype(BF16), ffn_w_out[0].astype(BF16), ln_g[0, 1], ln_b[0, 1], n)

    (sh1, sc1, g1, sh2, sc2, g2), _ = mod_vectors(1)
    b_s_full = jnp.repeat(jnp.transpose(gm_b_s[0]), GROUP_DIM, axis=1)
    uv = _gmlp_gate(h, sh1, sc1, gm_w_in[0].astype(BF16), gm_ln_g[0], gm_ln_b[0], gm_w_s[0].astype(BF16),
                    b_s_full, n)
    h = _proj_res_ln(uv, gm_w_out[0].astype(BF16), h, g1, ln_g[1, 0], ln_b[1, 0], n)
    h = _ffn(h, sh2, sc2, g2, ffn_w_in[1].astype(BF16), ffn_w_out[1].astype(BF16), ln_g[1, 1], ln_b[1, 1], n)
    return h.reshape(bsz, n, d)
```

```python
import functools

import numpy as np
import jax
import jax.numpy as jnp
from jax import lax
from jax.experimental import pallas as pl
from jax.experimental.pallas import tpu as pltpu

D_MODEL = 2048
GRID_W = 64
N_HEADS = 16
HEAD_DIM = D_MODEL // N_HEADS
WIN_H = 8
WIN_W = 16
CHUNK = 128
N_GROUPS = 16
GROUP_DIM = D_MODEL // N_GROUPS
DEPTH = 2
ALPHA = (2 * DEPTH) ** 0.25
LN_EPS = 1e-5

VMEM_LIMIT_BYTES = 60 * 1024 * 1024
MASK_VALUE = -1e30

Q_ROWS = 2
KEY_ROWS = Q_ROWS + WIN_H - 1
Q_BLK = Q_ROWS * GRID_W
K_BLK = KEY_ROWS * GRID_W
PROJ_SUB_BLOCKS = 4
ATTN_GROUP = 4

BF16 = jnp.bfloat16
F32 = jnp.float32


def _params(*sem):
    return pltpu.CompilerParams(dimension_semantics=sem, vmem_limit_bytes=VMEM_LIMIT_BYTES)


def _layer_norm(x, g, b):
    mu = jnp.mean(x, axis=-1, keepdims=True)
    xc = x - mu
    var = jnp.mean(xc * xc, axis=-1, keepdims=True)
    return xc * lax.rsqrt(var + LN_EPS) * g + b


def _silu(x):
    return x * (1.0 / (1.0 + jnp.exp(-x)))


def _dot(a, b):
    return jnp.dot(a, b, preferred_element_type=F32)


def _ada_kernel(cond_ref, w_ref, b_ref, o_ref):
    s = _silu(cond_ref[...]).astype(BF16)
    o_ref[0] = _dot(s, w_ref[0].astype(BF16)) + b_ref[0]


def _ada_params(cond, ada_w, ada_b, tn=1024):
    depth, d, n6 = ada_w.shape
    rows = cond.shape[0]
    return pl.pallas_call(
        _ada_kernel,
        grid=(depth, n6 // tn),
        in_specs=[
            pl.BlockSpec((rows, d), lambda l, j: (0, 0)),
            pl.BlockSpec((1, d, tn), lambda l, j: (l, 0, j)),
            pl.BlockSpec((1, 1, tn), lambda l, j: (l, 0, j)),
        ],
        out_specs=pl.BlockSpec((1, rows, tn), lambda l, j: (l, 0, j)),
        out_shape=jax.ShapeDtypeStruct((depth, rows, n6), F32),
        compiler_params=_params("parallel", "parallel"),
        name="ada_params",
    )(cond, ada_w, ada_b.reshape(depth, 1, n6))


def _mod_matmul_kernel(x_ref, sh_ref, sc_ref, w_ref, o_ref, a_ref, *, scaled_blocks, out_scale):
    @pl.when(pl.program_id(1) == 0)
    def _():
        a_ref[...] = (x_ref[...] * (1.0 + sc_ref[0]) + sh_ref[0]).astype(BF16)

    y = _dot(a_ref[...], w_ref[...])
    if scaled_blocks:
        y = y * jnp.where(pl.program_id(1) < scaled_blocks, out_scale, 1.0)
    o_ref[...] = y.astype(o_ref.dtype)


def _mod_matmul(x, sh, sc, w, rows_per_batch, tm, tn, col_start=0, scaled_cols=0, out_scale=1.0, name="mod_matmul"):
    m, d = x.shape
    n = w.shape[1] - col_start
    col_blk = col_start // tn
    tiles_per_batch = rows_per_batch // tm
    kern = functools.partial(_mod_matmul_kernel, scaled_blocks=scaled_cols // tn, out_scale=out_scale)
    if sh.shape[0] == 1:
        mod_idx = lambda i, j: (0, 0, 0)
    else:
        mod_idx = lambda i, j: (i // tiles_per_batch, 0, 0)
    return pl.pallas_call(
        kern,
        grid=(m // tm, n // tn),
        in_specs=[
            pl.BlockSpec((tm, d), lambda i, j: (i, 0)),
            pl.BlockSpec((1, 1, d), mod_idx),
            pl.BlockSpec((1, 1, d), mod_idx),
            pl.BlockSpec((d, tn), lambda i, j: (0, col_blk + j)),
        ],
        out_specs=pl.BlockSpec((tm, tn), lambda i, j: (i, j)),
        out_shape=jax.ShapeDtypeStruct((m, n), BF16),
        scratch_shapes=[pltpu.VMEM((tm, d), BF16)],
        compiler_params=_params("parallel", "arbitrary"),
        name=name,
    )(x, sh, sc, w)


def _attn_bias_tables(rpb, rows):
    n_blocks = rows // Q_ROWS
    last_kb = rows - KEY_ROWS
    geoms = [(0, 0), (Q_ROWS, 0), (2 * Q_ROWS, 0),
             ((n_blocks - 2) * Q_ROWS, last_kb), ((n_blocks - 1) * Q_ROWS, last_kb)]
    n_heads, n_row_rel, _ = rpb.shape
    period = 2 * GRID_W + 1
    pad = jnp.zeros((n_heads, n_row_rel, period - (2 * WIN_W - 1)), rpb.dtype)
    base = jnp.concatenate([rpb[..., WIN_W - 1:], pad, rpb[..., :WIN_W - 1]], axis=-1)
    toep = jnp.tile(base, (1, 1, GRID_W))[..., :GRID_W * (period - 1)]
    toep = toep.reshape(n_heads, n_row_rel, GRID_W, period - 1)[..., :GRID_W]

    c = np.arange(GRID_W)[:, None]
    kc = np.arange(GRID_W)[None, :]
    c0 = np.clip(c - WIN_W // 2, 0, GRID_W - WIN_W)
    col_ok = (kc >= c0) & (kc < c0 + WIN_W)
    masked = jnp.full((n_heads, GRID_W, GRID_W), MASK_VALUE, rpb.dtype)
    tiles = jnp.where(col_ok[None, None], toep, MASK_VALUE)
    tables = []
    for rb, kb in geoms:
        q_rows = []
        for dr in range(Q_ROWS):
            r = rb + dr
            r0 = min(max(r - WIN_H // 2, 0), rows - WIN_H)
            blocks = [tiles[:, kb + i - r + WIN_H - 1] if r0 <= kb + i < r0 + WIN_H else masked
                      for i in range(KEY_ROWS)]
            q_rows.append(jnp.concatenate(blocks, axis=-1))
        tables.append(jnp.concatenate(q_rows, axis=1))
    return jnp.stack(tables)


def _attn_kernel(q_ref, k_ref, v_ref, kc_ref, vc_ref, bias_ref, o_ref,
                 s0, s1, sc0, sc1, p0, p1, pc0, pc1, den0, den1, *, n_blocks, last_kb):
    s_bufs, sc_bufs, p_bufs, pc_bufs, den_bufs = (s0, s1), (sc0, sc1), (p0, p1), (pc0, pc1), (den0, den1)
    n_groups = n_blocks // ATTN_GROUP
    g_rows = ATTN_GROUP * Q_BLK
    contract_last = (((1,), (1,)), ((), ()))

    def key_start(t):
        kb = jnp.clip(Q_ROWS * t - WIN_H // 2, 0, last_kb)
        return pl.multiple_of(kb * GRID_W, GRID_W)

    def scores(g, slot):
        row0 = pl.multiple_of(g * g_rows, g_rows)
        q = q_ref[0, pl.ds(row0, g_rows), :]
        sc_bufs[slot][...] = lax.dot_general(q, kc_ref[0], contract_last, preferred_element_type=F32)
        for i in range(ATTN_GROUP):
            t = g * ATTN_GROUP + i
            geom = jnp.where(t < 2, t, jnp.where(t >= n_blocks - 2, t - (n_blocks - 5), 2))
            k = k_ref[0, pl.ds(key_start(t), K_BLK), :]
            s = lax.dot_general(q[i * Q_BLK:(i + 1) * Q_BLK], k, contract_last, preferred_element_type=F32)
            s_bufs[slot][i * Q_BLK:(i + 1) * Q_BLK, :] = s + bias_ref[geom, 0]

    def softmax(slot):
        s = s_bufs[slot][...]
        sc = sc_bufs[slot][...]
        m = jnp.maximum(jnp.max(s, axis=-1, keepdims=True), jnp.max(sc, axis=-1, keepdims=True))
        p = jnp.exp(s - m)
        pc = jnp.exp(sc - m)
        den_bufs[slot][...] = jnp.sum(p, axis=-1, keepdims=True) + jnp.sum(pc, axis=-1, keepdims=True)
        p_bufs[slot][...] = p.astype(BF16)
        pc_bufs[slot][...] = pc.astype(BF16)

    def weighted_sum(g, slot):
        row0 = pl.multiple_of(g * g_rows, g_rows)
        o_ctx = _dot(pc_bufs[slot][...], vc_ref[0])
        outs = []
        for i in range(ATTN_GROUP):
            v = v_ref[0, pl.ds(key_start(g * ATTN_GROUP + i), K_BLK), :]
            outs.append(_dot(p_bufs[slot][i * Q_BLK:(i + 1) * Q_BLK, :], v))
        o = (jnp.concatenate(outs, axis=0) + o_ctx) / den_bufs[slot][...]
        o_ref[0, pl.ds(row0, g_rows), :] = o.astype(o_ref.dtype)

    scores(0, 0)
    scores(1, 1)
    softmax(0)

    def steady(j, carry):
        for parity in range(2):
            tau = 2 * j + 2 + parity
            scores(tau, parity)
            softmax(1 - parity)
            weighted_sum(tau - 2, parity)
        return carry

    lax.fori_loop(0, (n_groups - 2) // 2, steady, 0)
    softmax(1)
    weighted_sum(n_groups - 2, 0)
    weighted_sum(n_groups - 1, 1)


def _neighbourhood_attention(qkv, kvc, bias):
    bsz, n, _ = qkv.shape
    ctx_len = kvc.shape[1]
    rows = n // GRID_W
    n_blocks = rows // Q_ROWS
    assert n_blocks % (2 * ATTN_GROUP) == 0 and n_blocks // ATTN_GROUP >= 4
    kern = functools.partial(_attn_kernel, n_blocks=n_blocks, last_kb=rows - KEY_ROWS)
    seq_spec = lambda off: pl.BlockSpec((1, n, HEAD_DIM), lambda h, b: (b, 0, off + h))
    ctx_spec = lambda off: pl.BlockSpec((1, ctx_len, HEAD_DIM), lambda h, b: (b, 0, off + h))
    g_rows = ATTN_GROUP * Q_BLK
    slot_pair = lambda cols, dtype: [pltpu.VMEM((g_rows, cols), dtype)] * 2
    return pl.pallas_call(
        kern,
        grid=(N_HEADS, bsz),
        in_specs=[
            seq_spec(0), seq_spec(N_HEADS), seq_spec(2 * N_HEADS),
            ctx_spec(0), ctx_spec(N_HEADS),
            pl.BlockSpec((bias.shape[0], 1, Q_BLK, K_BLK), lambda h, b: (0, h, 0, 0)),
        ],
        out_specs=pl.BlockSpec((1, n, HEAD_DIM), lambda h, b: (b, 0, h)),
        out_shape=jax.ShapeDtypeStruct((bsz, n, D_MODEL), BF16),
        scratch_shapes=(slot_pair(K_BLK, F32) + slot_pair(ctx_len, F32) + slot_pair(K_BLK, BF16)
                        + slot_pair(ctx_len, BF16) + slot_pair(1, F32)),
        compiler_params=_params("parallel", "parallel"),
        name="neighbourhood_attention",
    )(qkv, qkv, qkv, kvc, kvc, bias)


def _proj_res_ln_kernel(y_ref, w_ref, h_ref, g_ref, lg_ref, lb_ref, o_ref):
    sub = y_ref.shape[0] // PROJ_SUB_BLOCKS
    for r in range(PROJ_SUB_BLOCKS):
        rows = slice(r * sub, (r + 1) * sub)
        y = _dot(y_ref[rows, :], w_ref[...])
        o_ref[rows, :] = _layer_norm(ALPHA * h_ref[rows, :] + g_ref[0] * y, lg_ref[...], lb_ref[...])


def _proj_res_ln(y, w, h, gate, ln_g, ln_b, rows_per_batch, tm=1024):
    m, k = y.shape
    d = w.shape[1]
    tiles_per_batch = rows_per_batch // tm
    return pl.pallas_call(
        _proj_res_ln_kernel,
        grid=(m // tm,),
        in_specs=[
            pl.BlockSpec((tm, k), lambda i: (i, 0)),
            pl.BlockSpec((k, d), lambda i: (0, 0), pipeline_mode=pl.Buffered(1)),
            pl.BlockSpec((tm, d), lambda i: (i, 0)),
            pl.BlockSpec((1, 1, d), lambda i: (i // tiles_per_batch, 0, 0)),
            pl.BlockSpec((1, d), lambda i: (0, 0)),
            pl.BlockSpec((1, d), lambda i: (0, 0)),
        ],
        out_specs=pl.BlockSpec((tm, d), lambda i: (i, 0)),
        out_shape=jax.ShapeDtypeStruct((m, d), F32),
        compiler_params=_params("parallel"),
        name="proj_res_ln",
    )(y, w, h, gate, ln_g.reshape(1, d), ln_b.reshape(1, d))


def _ffn_kernel(h_ref, sh_ref, sc_ref, g_ref, wg_ref, wu_ref, wo_ref, lg_ref, lb_ref, o_ref, a_ref):
    f = pl.program_id(1)

    @pl.when(f == 0)
    def _():
        h = h_ref[...]
        a_ref[...] = (h * (1.0 + sc_ref[0]) + sh_ref[0]).astype(BF16)
        o_ref[...] = ALPHA * h

    a = a_ref[...]
    act = _silu(_dot(a, wg_ref[...])) * _dot(a, wu_ref[...])
    o_ref[...] += g_ref[0] * _dot(act.astype(BF16), wo_ref[...])

    @pl.when(f == pl.num_programs(1) - 1)
    def _():
        o_ref[...] = _layer_norm(o_ref[...], lg_ref[...], lb_ref[...])


def _ffn(h, sh, sc, gate, w_in, w_out, layer, ln_g, ln_b, rows_per_batch, tm=1024, tf=512):
    m, d = h.shape
    d_ff = w_out.shape[1]
    n_f = d_ff // tf
    tiles_per_batch = rows_per_batch // tm
    mod_spec = pl.BlockSpec((1, 1, d), lambda i, f: (i // tiles_per_batch, 0, 0))
    vec_spec = pl.BlockSpec((1, d), lambda i, f: (0, 0))
    return pl.pallas_call(
        _ffn_kernel,
        grid=(m // tm, n_f),
        in_specs=[
            pl.BlockSpec((tm, d), lambda i, f: (i, 0)),
            mod_spec, mod_spec, mod_spec,
            pl.BlockSpec((None, d, tf), lambda i, f: (layer, 0, f)),
            pl.BlockSpec((None, d, tf), lambda i, f: (layer, 0, n_f + f)),
            pl.BlockSpec((None, tf, d), lambda i, f: (layer, f, 0)),
            vec_spec, vec_spec,
        ],
        out_specs=pl.BlockSpec((tm, d), lambda i, f: (i, 0)),
        out_shape=jax.ShapeDtypeStruct((m, d), F32),
        scratch_shapes=[pltpu.VMEM((tm, d), BF16)],
        compiler_params=_params("parallel", "arbitrary"),
        name="swiglu_ffn",
    )(h, sh, sc, gate, w_in, w_in, w_out, ln_g.reshape(1, d), ln_b.reshape(1, d))


def _gmlp_kernel(h_ref, sh_ref, sc_ref, wu_ref, wv_ref, lg_ref, lb_ref, ws_ref, bs_ref, o_ref, u_ref, v_ref):
    a = (h_ref[...] * (1.0 + sc_ref[0]) + sh_ref[0]).astype(BF16)

    def gelu(z):
        return 0.5 * z * (1.0 + lax.erf(z * (2.0 ** -0.5)))

    u_ref[...] = gelu(_dot(a, wu_ref[...]))
    v_ref[...] = _layer_norm(gelu(_dot(a, wv_ref[...])), lg_ref[...], lb_ref[...]).astype(BF16)
    for ci in range(h_ref.shape[0] // CHUNK):
        rows = slice(ci * CHUNK, (ci + 1) * CHUNK)
        for g in range(N_GROUPS):
            cols = slice(g * GROUP_DIM, (g + 1) * GROUP_DIM)
            mixed = _dot(ws_ref[g], v_ref[rows, cols]) + bs_ref[:, cols]
            o_ref[rows, cols] = (u_ref[rows, cols] * mixed).astype(o_ref.dtype)


def _gmlp_gate(h, sh, sc, w_in, ln_g, ln_b, w_s, b_s_full, rows_per_batch, tm=256):
    m, d = h.shape
    tiles_per_batch = rows_per_batch // tm
    mod_spec = pl.BlockSpec((1, 1, d), lambda i: (i // tiles_per_batch, 0, 0))
    vec_spec = pl.BlockSpec((1, d), lambda i: (0, 0))
    return pl.pallas_call(
        _gmlp_kernel,
        grid=(m // tm,),
        in_specs=[
            pl.BlockSpec((tm, d), lambda i: (i, 0)),
            mod_spec, mod_spec,
            pl.BlockSpec((d, d), lambda i: (0, 0)),
            pl.BlockSpec((d, d), lambda i: (0, 1)),
            vec_spec, vec_spec,
            pl.BlockSpec((N_GROUPS, CHUNK, CHUNK), lambda i: (0, 0, 0)),
            pl.BlockSpec((CHUNK, d), lambda i: (0, 0)),
        ],
        out_specs=pl.BlockSpec((tm, d), lambda i: (i, 0)),
        out_shape=jax.ShapeDtypeStruct((m, d), BF16),
        scratch_shapes=[pltpu.VMEM((tm, d), F32), pltpu.VMEM((tm, d), BF16)],
        compiler_params=_params("parallel"),
        name="gmlp_gate",
    )(h, sh, sc, w_in, w_in, ln_g.reshape(1, d), ln_b.reshape(1, d), w_s, b_s_full)


def kernel(x, c, ctx, c_ctx, ada_w, ada_b, ln_g, ln_b, na_w_qkv, na_w_o, na_rpb, gm_w_in, gm_ln_g, gm_ln_b,
           gm_w_s, gm_b_s, gm_w_out, ffn_w_in, ffn_w_out):
    bsz, n, d = x.shape
    ctx_len = ctx.shape[1]
    rows = n // GRID_W
    assert d == D_MODEL and n % (GRID_W * Q_ROWS) == 0 and rows >= KEY_ROWS + 2 * Q_ROWS

    cond = jnp.concatenate([c, c_ctx[None, :], jnp.zeros((8 - bsz - 1, d), F32)], axis=0)
    ada = _ada_params(cond, ada_w, ada_b)

    def mod_vectors(layer):
        parts = jnp.split(ada[layer], 6, axis=-1)
        latent = [p[:bsz].reshape(bsz, 1, d) for p in parts]
        context = [p[bsz:bsz + 1].reshape(1, 1, d) for p in parts]
        return latent, context

    h = x.reshape(bsz * n, d)

    (sh1, sc1, g1, sh2, sc2, g2), (csh1, csc1, _, _, _, _) = mod_vectors(0)
    w_qkv = na_w_qkv[0].astype(BF16)
    ffn_w_in_bf = ffn_w_in.astype(BF16)
    ffn_w_out_bf = ffn_w_out.astype(BF16)
    qkv = _mod_matmul(h, sh1, sc1, w_qkv, n, tm=1024, tn=1024, scaled_cols=d, out_scale=HEAD_DIM ** -0.5,
                      name="qkv_proj")
    kvc = _mod_matmul(ctx.reshape(bsz * ctx_len, d), csh1, csc1, w_qkv, ctx_len, tm=ctx_len, tn=1024, col_start=d,
                      name="ctx_kv_proj")
    bias = _attn_bias_tables(na_rpb[0], rows)
    o = _neighbourhood_attention(qkv.reshape(bsz, n, 3 * d), kvc.reshape(bsz, ctx_len, 2 * d), bias)
    h = _proj_res_ln(o.reshape(bsz * n, d), na_w_o[0].astype(BF16), h, g1, ln_g[0, 0], ln_b[0, 0], n)
    h = _ffn(h, sh2, sc2, g2, ffn_w_in_bf, ffn_w_out_bf, 0, ln_g[0, 1], ln_b[0, 1], n)

    (sh1, sc1, g1, sh2, sc2, g2), _ = mod_vectors(1)
    b_s_full = jnp.repeat(jnp.transpose(gm_b_s[0]), GROUP_DIM, axis=1)
    uv = _gmlp_gate(h, sh1, sc1, gm_w_in[0].astype(BF16), gm_ln_g[0], gm_ln_b[0], gm_w_s[0].astype(BF16),
                    b_s_full, n)
    h = _proj_res_ln(uv, gm_w_out[0].astype(BF16), h, g1, ln_g[1, 0], ln_b[1, 0], n)
    h = _ffn(h, sh2, sc2, g2, ffn_w_in_bf, ffn_w_out_bf, 1, ln_g[1, 1], ln_b[1, 1], n)
    return h.reshape(bsz, n, d)
```

```python
import functools
from typing import NamedTuple

import numpy as np
import jax
import jax.numpy as jnp
from jax import lax
from jax.experimental import pallas as pl
from jax.experimental.pallas import tpu as pltpu

D_MODEL = 2048
GRID_W = 64
N_HEADS = 16
HEAD_DIM = D_MODEL // N_HEADS
WIN_H = 8
WIN_W = 16
CHUNK = 128
N_GROUPS = 16
GROUP_DIM = D_MODEL // N_GROUPS
DEPTH = 2
ALPHA = (2 * DEPTH) ** 0.25
LN_EPS = 1e-5

VMEM_LIMIT_BYTES = 60 * 1024 * 1024
MASK_VALUE = -1e30

Q_ROWS = 2
KEY_ROWS = Q_ROWS + WIN_H - 1
Q_BLK = Q_ROWS * GRID_W
K_BLK = KEY_ROWS * GRID_W
GMLP_COL_CHUNK = 512
PROJ_SUB_BLOCKS = 4
ATTN_GROUP = 4

BF16 = jnp.bfloat16
F32 = jnp.float32
BF16_SUBLANES = 16


def _params(*sem):
    return pltpu.CompilerParams(dimension_semantics=sem, vmem_limit_bytes=VMEM_LIMIT_BYTES)


def _layer_norm(x, g, b):
    mu = jnp.mean(x, axis=-1, keepdims=True)
    xc = x - mu
    var = jnp.mean(xc * xc, axis=-1, keepdims=True)
    return xc * lax.rsqrt(var + LN_EPS) * g + b


def _silu(x):
    return x * (1.0 / (1.0 + jnp.exp(-x)))


def _dot(a, b):
    return jnp.dot(a, b, preferred_element_type=F32)


class _CastJob(NamedTuple):
    src: jax.Array
    layer: int
    rows: int
    steps_per_block: int


def _cast_job(src, layer, n_steps):
    r = src.shape[1]
    for steps_per_block in (1, 2, 4, 8):
        rows, rem = divmod(r * steps_per_block, n_steps)
        if rem == 0 and rows % BF16_SUBLANES == 0:
            return _CastJob(src, layer, rows, steps_per_block)
    raise ValueError(f"cannot split {r} rows over {n_steps} steps")


def _cast_io(jobs, step_of):
    in_specs, out_specs, out_shapes = [], [], []
    for job in jobs:
        _, r, c = job.src.shape
        in_specs.append(pl.BlockSpec(
            (None, job.rows, c), lambda *g, job=job: (job.layer, step_of(*g) // job.steps_per_block, 0)))
        out_specs.append(pl.BlockSpec(
            (job.rows, c), lambda *g, job=job: (step_of(*g) // job.steps_per_block, 0)))
        out_shapes.append(jax.ShapeDtypeStruct((r, c), BF16))
    return in_specs, out_specs, out_shapes


def _run_casts(src_refs, dst_refs):
    for src, dst in zip(src_refs, dst_refs):
        dst[...] = src[...].astype(dst.dtype)


def _ada_kernel(cond_ref, w_ref, b_ref, o_ref):
    s = _silu(cond_ref[...]).astype(BF16)
    o_ref[0] = _dot(s, w_ref[0].astype(BF16)) + b_ref[0]


def _ada_params(cond, ada_w, ada_b, tn=1024):
    depth, d, n6 = ada_w.shape
    rows = cond.shape[0]
    return pl.pallas_call(
        _ada_kernel,
        grid=(depth, n6 // tn),
        in_specs=[
            pl.BlockSpec((rows, d), lambda l, j: (0, 0)),
            pl.BlockSpec((1, d, tn), lambda l, j: (l, 0, j)),
            pl.BlockSpec((1, 1, tn), lambda l, j: (l, 0, j)),
        ],
        out_specs=pl.BlockSpec((1, rows, tn), lambda l, j: (l, 0, j)),
        out_shape=jax.ShapeDtypeStruct((depth, rows, n6), F32),
        compiler_params=_params("parallel", "parallel"),
        name="ada_params",
    )(cond, ada_w, ada_b.reshape(depth, 1, n6))


def _mod_matmul_kernel(*refs, n_cast):
    x_ref, sh_ref, sc_ref, w_ref, cs_ref = refs[:5]
    cast_src = refs[5:5 + n_cast]
    o_ref = refs[5 + n_cast]
    cast_dst = refs[6 + n_cast:6 + 2 * n_cast]
    a_ref = refs[6 + 2 * n_cast]

    @pl.when(pl.program_id(1) == 0)
    def _():
        a_ref[...] = (x_ref[...] * (1.0 + sc_ref[0]) + sh_ref[0]).astype(BF16)

    o_ref[...] = (_dot(a_ref[...], w_ref[...]) * cs_ref[...]).astype(o_ref.dtype)
    _run_casts(cast_src, cast_dst)


def _mod_matmul(x, sh, sc, w, col_scale, rows_per_batch, tm, tn, col_start=0, cast_srcs=(), name="mod_matmul"):
    m, d = x.shape
    n = w.shape[1] - col_start
    col_blk = col_start // tn
    tiles_per_batch = rows_per_batch // tm
    grid = (m // tm, n // tn)
    jobs = [_cast_job(src, layer, grid[0] * grid[1]) for src, layer in cast_srcs]
    cast_in, cast_out, cast_shapes = _cast_io(jobs, lambda i, j: i * grid[1] + j)
    if sh.shape[0] == 1:
        mod_idx = lambda i, j: (0, 0, 0)
    else:
        mod_idx = lambda i, j: (i // tiles_per_batch, 0, 0)
    return pl.pallas_call(
        functools.partial(_mod_matmul_kernel, n_cast=len(jobs)),
        grid=grid,
        in_specs=[
            pl.BlockSpec((tm, d), lambda i, j: (i, 0)),
            pl.BlockSpec((1, 1, d), mod_idx),
            pl.BlockSpec((1, 1, d), mod_idx),
            pl.BlockSpec((d, tn), lambda i, j: (0, col_blk + j)),
            pl.BlockSpec((1, tn), lambda i, j: (0, j)),
        ] + cast_in,
        out_specs=[pl.BlockSpec((tm, tn), lambda i, j: (i, j))] + cast_out,
        out_shape=[jax.ShapeDtypeStruct((m, n), BF16)] + cast_shapes,
        scratch_shapes=[pltpu.VMEM((tm, d), BF16)],
        compiler_params=_params("parallel", "arbitrary"),
        name=name,
    )(x, sh, sc, w, col_scale, *[job.src for job in jobs])


def _attn_bias_tables(rpb, rows):
    n_blocks = rows // Q_ROWS
    last_kb = rows - KEY_ROWS
    geoms = [(0, 0), (Q_ROWS, 0), (2 * Q_ROWS, 0),
             ((n_blocks - 2) * Q_ROWS, last_kb), ((n_blocks - 1) * Q_ROWS, last_kb)]
    n_heads, n_row_rel, _ = rpb.shape
    period = 2 * GRID_W + 1
    pad = jnp.zeros((n_heads, n_row_rel, period - (2 * WIN_W - 1)), rpb.dtype)
    base = jnp.concatenate([rpb[..., WIN_W - 1:], pad, rpb[..., :WIN_W - 1]], axis=-1)
    toep = jnp.tile(base, (1, 1, GRID_W))[..., :GRID_W * (period - 1)]
    toep = toep.reshape(n_heads, n_row_rel, GRID_W, period - 1)[..., :GRID_W]

    c = np.arange(GRID_W)[:, None]
    kc = np.arange(GRID_W)[None, :]
    c0 = np.clip(c - WIN_W // 2, 0, GRID_W - WIN_W)
    col_ok = (kc >= c0) & (kc < c0 + WIN_W)
    masked = jnp.full((n_heads, GRID_W, GRID_W), MASK_VALUE, rpb.dtype)
    tiles = jnp.where(col_ok[None, None], toep, MASK_VALUE)
    tables = []
    for rb, kb in geoms:
        q_rows = []
        for dr in range(Q_ROWS):
            r = rb + dr
            r0 = min(max(r - WIN_H // 2, 0), rows - WIN_H)
            blocks = [tiles[:, kb + i - r + WIN_H - 1] if r0 <= kb + i < r0 + WIN_H else masked
                      for i in range(KEY_ROWS)]
            q_rows.append(jnp.concatenate(blocks, axis=-1))
        tables.append(jnp.concatenate(q_rows, axis=1))
    return jnp.stack(tables)


def _attn_kernel(*refs, n_blocks, last_kb, n_cast):
    q_ref, k_ref, v_ref, kc_ref, vc_ref, bias_ref = refs[:6]
    o_ref = refs[6 + n_cast]
    s0, s1, sc0, sc1, p0, p1, pc0, pc1, den0, den1 = refs[7 + 2 * n_cast:]
    _run_casts(refs[6:6 + n_cast], refs[7 + n_cast:7 + 2 * n_cast])
    _attn_pipeline(q_ref, k_ref, v_ref, kc_ref, vc_ref, bias_ref, o_ref,
                   s0, s1, sc0, sc1, p0, p1, pc0, pc1, den0, den1, n_blocks=n_blocks, last_kb=last_kb)


def _attn_pipeline(q_ref, k_ref, v_ref, kc_ref, vc_ref, bias_ref, o_ref,
                   s0, s1, sc0, sc1, p0, p1, pc0, pc1, den0, den1, *, n_blocks, last_kb):
    s_bufs, sc_bufs, p_bufs, pc_bufs, den_bufs = (s0, s1), (sc0, sc1), (p0, p1), (pc0, pc1), (den0, den1)
    n_groups = n_blocks // ATTN_GROUP
    g_rows = ATTN_GROUP * Q_BLK
    contract_last = (((1,), (1,)), ((), ()))

    def key_start(t):
        kb = jnp.clip(Q_ROWS * t - WIN_H // 2, 0, last_kb)
        return pl.multiple_of(kb * GRID_W, GRID_W)

    def scores(g, slot):
        row0 = pl.multiple_of(g * g_rows, g_rows)
        q = q_ref[0, pl.ds(row0, g_rows), :]
        sc_bufs[slot][...] = lax.dot_general(q, kc_ref[0], contract_last, preferred_element_type=F32)
        for i in range(ATTN_GROUP):
            t = g * ATTN_GROUP + i
            geom = jnp.where(t < 2, t, jnp.where(t >= n_blocks - 2, t - (n_blocks - 5), 2))
            k = k_ref[0, pl.ds(key_start(t), K_BLK), :]
            s = lax.dot_general(q[i * Q_BLK:(i + 1) * Q_BLK], k, contract_last, preferred_element_type=F32)
            s_bufs[slot][i * Q_BLK:(i + 1) * Q_BLK, :] = s + bias_ref[geom, 0]

    def softmax(slot):
        s = s_bufs[slot][...]
        sc = sc_bufs[slot][...]
        m = jnp.maximum(jnp.max(s, axis=-1, keepdims=True), jnp.max(sc, axis=-1, keepdims=True))
        p = jnp.exp(s - m)
        pc = jnp.exp(sc - m)
        den_bufs[slot][...] = jnp.sum(p, axis=-1, keepdims=True) + jnp.sum(pc, axis=-1, keepdims=True)
        p_bufs[slot][...] = p.astype(BF16)
        pc_bufs[slot][...] = pc.astype(BF16)

    def weighted_sum(g, slot):
        row0 = pl.multiple_of(g * g_rows, g_rows)
        o_ctx = _dot(pc_bufs[slot][...], vc_ref[0])
        outs = []
        for i in range(ATTN_GROUP):
            v = v_ref[0, pl.ds(key_start(g * ATTN_GROUP + i), K_BLK), :]
            outs.append(_dot(p_bufs[slot][i * Q_BLK:(i + 1) * Q_BLK, :], v))
        o = (jnp.concatenate(outs, axis=0) + o_ctx) / den_bufs[slot][...]
        o_ref[0, pl.ds(row0, g_rows), :] = o.astype(o_ref.dtype)

    scores(0, 0)
    scores(1, 1)
    softmax(0)

    def steady(j, carry):
        for parity in range(2):
            tau = 2 * j + 2 + parity
            scores(tau, parity)
            softmax(1 - parity)
            weighted_sum(tau - 2, parity)
        return carry

    lax.fori_loop(0, (n_groups - 2) // 2, steady, 0)
    softmax(1)
    weighted_sum(n_groups - 2, 0)
    weighted_sum(n_groups - 1, 1)


def _neighbourhood_attention(qkv, kvc, bias, cast_srcs=()):
    bsz, n, _ = qkv.shape
    ctx_len = kvc.shape[1]
    rows = n // GRID_W
    n_blocks = rows // Q_ROWS
    assert n_blocks % (2 * ATTN_GROUP) == 0 and n_blocks // ATTN_GROUP >= 4
    jobs = [_cast_job(src, layer, N_HEADS * bsz) for src, layer in cast_srcs]
    cast_in, cast_out, cast_shapes = _cast_io(jobs, lambda h, b: h * bsz + b)
    kern = functools.partial(_attn_kernel, n_blocks=n_blocks, last_kb=rows - KEY_ROWS, n_cast=len(jobs))
    seq_spec = lambda off: pl.BlockSpec((1, n, HEAD_DIM), lambda h, b: (b, 0, off + h))
    ctx_spec = lambda off: pl.BlockSpec((1, ctx_len, HEAD_DIM), lambda h, b: (b, 0, off + h))
    g_rows = ATTN_GROUP * Q_BLK
    slot_pair = lambda cols, dtype: [pltpu.VMEM((g_rows, cols), dtype)] * 2
    return pl.pallas_call(
        kern,
        grid=(N_HEADS, bsz),
        in_specs=[
            seq_spec(0), seq_spec(N_HEADS), seq_spec(2 * N_HEADS),
            ctx_spec(0), ctx_spec(N_HEADS),
            pl.BlockSpec((bias.shape[0], 1, Q_BLK, K_BLK), lambda h, b: (0, h, 0, 0)),
        ] + cast_in,
        out_specs=[pl.BlockSpec((1, n, HEAD_DIM), lambda h, b: (b, 0, h))] + cast_out,
        out_shape=[jax.ShapeDtypeStruct((bsz, n, D_MODEL), BF16)] + cast_shapes,
        scratch_shapes=(slot_pair(K_BLK, F32) + slot_pair(ctx_len, F32) + slot_pair(K_BLK, BF16)
                        + slot_pair(ctx_len, BF16) + slot_pair(1, F32)),
        compiler_params=_params("parallel", "parallel"),
        name="neighbourhood_attention",
    )(qkv, qkv, qkv, kvc, kvc, bias, *[job.src for job in jobs])


def _proj_res_ln_kernel(y_ref, w_ref, h_ref, g_ref, lg_ref, lb_ref, o_ref):
    sub = y_ref.shape[0] // PROJ_SUB_BLOCKS
    for r in range(PROJ_SUB_BLOCKS):
        rows = slice(r * sub, (r + 1) * sub)
        y = _dot(y_ref[rows, :], w_ref[...])
        o_ref[rows, :] = _layer_norm(ALPHA * h_ref[rows, :] + g_ref[0] * y, lg_ref[...], lb_ref[...])


def _proj_res_ln(y, w, h, gate, ln_g, ln_b, rows_per_batch, tm=1024):
    m, k = y.shape
    d = w.shape[1]
    tiles_per_batch = rows_per_batch // tm
    return pl.pallas_call(
        _proj_res_ln_kernel,
        grid=(m // tm,),
        in_specs=[
            pl.BlockSpec((tm, k), lambda i: (i, 0)),
            pl.BlockSpec((k, d), lambda i: (0, 0), pipeline_mode=pl.Buffered(1)),
            pl.BlockSpec((tm, d), lambda i: (i, 0)),
            pl.BlockSpec((1, 1, d), lambda i: (i // tiles_per_batch, 0, 0)),
            pl.BlockSpec((1, d), lambda i: (0, 0)),
            pl.BlockSpec((1, d), lambda i: (0, 0)),
        ],
        out_specs=pl.BlockSpec((tm, d), lambda i: (i, 0)),
        out_shape=jax.ShapeDtypeStruct((m, d), F32),
        compiler_params=_params("parallel"),
        name="proj_res_ln",
    )(y, w, h, gate, ln_g.reshape(1, d), ln_b.reshape(1, d))


def _ffn_kernel(h_ref, sh_ref, sc_ref, g_ref, wg_ref, wu_ref, wo_ref, lg_ref, lb_ref, o_ref, a_ref):
    f = pl.program_id(1)

    @pl.when(f == 0)
    def _():
        h = h_ref[...]
        a_ref[...] = (h * (1.0 + sc_ref[0]) + sh_ref[0]).astype(BF16)
        o_ref[...] = ALPHA * h

    a = a_ref[...]
    act = _silu(_dot(a, wg_ref[...])) * _dot(a, wu_ref[...])
    o_ref[...] += g_ref[0] * _dot(act.astype(BF16), wo_ref[...])

    @pl.when(f == pl.num_programs(1) - 1)
    def _():
        o_ref[...] = _layer_norm(o_ref[...], lg_ref[...], lb_ref[...])


def _ffn(h, sh, sc, gate, w_in, w_out, ln_g, ln_b, rows_per_batch, tm=1024, tf=512):
    m, d = h.shape
    d_ff = w_out.shape[0]
    n_f = d_ff // tf
    tiles_per_batch = rows_per_batch // tm
    mod_spec = pl.BlockSpec((1, 1, d), lambda i, f: (i // tiles_per_batch, 0, 0))
    vec_spec = pl.BlockSpec((1, d), lambda i, f: (0, 0))
    return pl.pallas_call(
        _ffn_kernel,
        grid=(m // tm, n_f),
        in_specs=[
            pl.BlockSpec((tm, d), lambda i, f: (i, 0)),
            mod_spec, mod_spec, mod_spec,
            pl.BlockSpec((d, tf), lambda i, f: (0, f)),
            pl.BlockSpec((d, tf), lambda i, f: (0, n_f + f)),
            pl.BlockSpec((tf, d), lambda i, f: (f, 0)),
            vec_spec, vec_spec,
        ],
        out_specs=pl.BlockSpec((tm, d), lambda i, f: (i, 0)),
        out_shape=jax.ShapeDtypeStruct((m, d), F32),
        scratch_shapes=[pltpu.VMEM((tm, d), BF16)],
        compiler_params=_params("parallel", "arbitrary"),
        name="swiglu_ffn",
    )(h, sh, sc, gate, w_in, w_in, w_out, ln_g.reshape(1, d), ln_b.reshape(1, d))


def _gmlp_kernel(*refs, n_cast):
    h_ref, sh_ref, sc_ref, wu_ref, wv_ref, lg_ref, lb_ref, ws_ref, bs_ref = refs[:9]
    o_ref = refs[9 + n_cast]
    u_ref, v_ref = refs[10 + 2 * n_cast:]
    _run_casts(refs[9:9 + n_cast], refs[10 + n_cast:10 + 2 * n_cast])
    a = (h_ref[...] * (1.0 + sc_ref[0]) + sh_ref[0]).astype(BF16)

    def gelu(z):
        return 0.5 * z * (1.0 + lax.erf(z * (2.0 ** -0.5)))

    d = u_ref.shape[1]
    col_chunks = [slice(c * GMLP_COL_CHUNK, (c + 1) * GMLP_COL_CHUNK) for c in range(d // GMLP_COL_CHUNK)]
    v = jnp.concatenate([gelu(_dot(a, wv_ref[:, cols])) for cols in col_chunks], axis=-1)
    v_ref[...] = _layer_norm(v, lg_ref[...], lb_ref[...]).astype(BF16)
    for cols in col_chunks:
        u_ref[:, cols] = gelu(_dot(a, wu_ref[:, cols]))
    for ci in range(h_ref.shape[0] // CHUNK):
        rows = slice(ci * CHUNK, (ci + 1) * CHUNK)
        for g in range(N_GROUPS):
            cols = slice(g * GROUP_DIM, (g + 1) * GROUP_DIM)
            mixed = _dot(ws_ref[g], v_ref[rows, cols]) + bs_ref[:, cols]
            o_ref[rows, cols] = (u_ref[rows, cols] * mixed).astype(o_ref.dtype)


def _gmlp_gate(h, sh, sc, w_in, ln_g, ln_b, w_s, b_s_full, rows_per_batch, tm=256, cast_srcs=()):
    m, d = h.shape
    tiles_per_batch = rows_per_batch // tm
    mod_spec = pl.BlockSpec((1, 1, d), lambda i: (i // tiles_per_batch, 0, 0))
    vec_spec = pl.BlockSpec((1, d), lambda i: (0, 0))
    jobs = [_cast_job(src, layer, m // tm) for src, layer in cast_srcs]
    cast_in, cast_out, cast_shapes = _cast_io(jobs, lambda i: i)
    return pl.pallas_call(
        functools.partial(_gmlp_kernel, n_cast=len(jobs)),
        grid=(m // tm,),
        in_specs=[
            pl.BlockSpec((tm, d), lambda i: (i, 0)),
            mod_spec, mod_spec,
            pl.BlockSpec((d, d), lambda i: (0, 0)),
            pl.BlockSpec((d, d), lambda i: (0, 1)),
            vec_spec, vec_spec,
            pl.BlockSpec((N_GROUPS, CHUNK, CHUNK), lambda i: (0, 0, 0)),
            pl.BlockSpec((CHUNK, d), lambda i: (0, 0)),
        ] + cast_in,
        out_specs=[pl.BlockSpec((tm, d), lambda i: (i, 0))] + cast_out,
        out_shape=[jax.ShapeDtypeStruct((m, d), BF16)] + cast_shapes,
        scratch_shapes=[pltpu.VMEM((tm, d), F32), pltpu.VMEM((tm, d), BF16)],
        compiler_params=_params("arbitrary"),
        name="gmlp_gate",
    )(h, sh, sc, w_in, w_in, ln_g.reshape(1, d), ln_b.reshape(1, d), w_s, b_s_full, *[job.src for job in jobs])


def kernel(x, c, ctx, c_ctx, ada_w, ada_b, ln_g, ln_b, na_w_qkv, na_w_o, na_rpb, gm_w_in, gm_ln_g, gm_ln_b,
           gm_w_s, gm_b_s, gm_w_out, ffn_w_in, ffn_w_out):
    bsz, n, d = x.shape
    ctx_len = ctx.shape[1]
    rows = n // GRID_W
    assert d == D_MODEL and n % (GRID_W * Q_ROWS) == 0 and rows >= KEY_ROWS + 2 * Q_ROWS

    cond = jnp.concatenate([c, c_ctx[None, :], jnp.zeros((8 - bsz - 1, d), F32)], axis=0)
    ada = _ada_params(cond, ada_w, ada_b)

    def mod_vectors(layer):
        parts = jnp.split(ada[layer], 6, axis=-1)
        latent = [p[:bsz].reshape(bsz, 1, d) for p in parts]
        context = [p[bsz:bsz + 1].reshape(1, 1, d) for p in parts]
        return latent, context

    h = x.reshape(bsz * n, d)

    (sh1, sc1, g1, sh2, sc2, g2), (csh1, csc1, _, _, _, _) = mod_vectors(0)
    w_qkv = na_w_qkv[0].astype(BF16)
    q_scale = jnp.concatenate([jnp.full((1, d), HEAD_DIM ** -0.5, F32), jnp.ones((1, 2 * d), F32)], axis=1)
    qkv, ffn_w_in0, ffn_w_out0 = _mod_matmul(h, sh1, sc1, w_qkv, q_scale, n, tm=1024, tn=768,
                                             cast_srcs=[(ffn_w_in, 0), (ffn_w_out, 0)], name="qkv_proj")
    kvc, = _mod_matmul(ctx.reshape(bsz * ctx_len, d), csh1, csc1, w_qkv, jnp.ones((1, 2 * d), F32), ctx_len,
                       tm=ctx_len, tn=1024, col_start=d, name="ctx_kv_proj")
    bias = _attn_bias_tables(na_rpb[0], rows)
    o, w_o, gm_w_in_bf, gm_w_out_bf = _neighbourhood_attention(
        qkv.reshape(bsz, n, 3 * d), kvc.reshape(bsz, ctx_len, 2 * d), bias,
        cast_srcs=[(na_w_o, 0), (gm_w_in, 0), (gm_w_out, 0)])
    h = _proj_res_ln(o.reshape(bsz * n, d), w_o, h, g1, ln_g[0, 0], ln_b[0, 0], n)
    h = _ffn(h, sh2, sc2, g2, ffn_w_in0, ffn_w_out0, ln_g[0, 1], ln_b[0, 1], n)

    (sh1, sc1, g1, sh2, sc2, g2), _ = mod_vectors(1)
    b_s_full = jnp.repeat(jnp.transpose(gm_b_s[0]), GROUP_DIM, axis=1)
    uv, ffn_w_in1, ffn_w_out1 = _gmlp_gate(h, sh1, sc1, gm_w_in_bf, gm_ln_g[0], gm_ln_b[0], gm_w_s[0].astype(BF16),
                                            b_s_full, n, cast_srcs=[(ffn_w_in, 1), (ffn_w_out, 1)])
    h = _proj_res_ln(uv, gm_w_out_bf, h, g1, ln_g[1, 0], ln_b[1, 0], n)
    h = _ffn(h, sh2, sc2, g2, ffn_w_in1, ffn_w_out1, ln_g[1, 1], ln_b[1, 1], n)
    return h.reshape(bsz, n, d)
```

```python
import functools
from typing import NamedTuple

import numpy as np
import jax
import jax.numpy as jnp
from jax import lax
from jax.experimental import pallas as pl
from jax.experimental.pallas import tpu as pltpu

D_MODEL = 2048
GRID_W = 64
N_HEADS = 16
HEAD_DIM = D_MODEL // N_HEADS
WIN_H = 8
WIN_W = 16
CHUNK = 128
N_GROUPS = 16
GROUP_DIM = D_MODEL // N_GROUPS
DEPTH = 2
ALPHA = (2 * DEPTH) ** 0.25
LN_EPS = 1e-5

VMEM_LIMIT_BYTES = 60 * 1024 * 1024
MASK_VALUE = -1e30

Q_ROWS = 2
KEY_ROWS = Q_ROWS + WIN_H - 1
Q_BLK = Q_ROWS * GRID_W
K_BLK = KEY_ROWS * GRID_W
GMLP_COL_CHUNK = 512
PROJ_SUB_BLOCKS = 4
ATTN_GROUP = 4
ATTN_ITEMS = 2

BF16 = jnp.bfloat16
F32 = jnp.float32
BF16_SUBLANES = 16


def _params(*sem):
    return pltpu.CompilerParams(dimension_semantics=sem, vmem_limit_bytes=VMEM_LIMIT_BYTES)


def _layer_norm(x, g, b):
    mu = jnp.mean(x, axis=-1, keepdims=True)
    xc = x - mu
    var = jnp.mean(xc * xc, axis=-1, keepdims=True)
    return xc * lax.rsqrt(var + LN_EPS) * g + b


def _silu(x):
    return x * (1.0 / (1.0 + jnp.exp(-x)))


def _dot(a, b):
    return jnp.dot(a, b, preferred_element_type=F32)


class _CastJob(NamedTuple):
    src: jax.Array
    layer: int
    rows: int
    steps_per_block: int


def _cast_job(src, layer, n_steps):
    r = src.shape[1]
    for steps_per_block in (1, 2, 4, 8):
        rows, rem = divmod(r * steps_per_block, n_steps)
        if rem == 0 and rows % BF16_SUBLANES == 0:
            return _CastJob(src, layer, rows, steps_per_block)
    raise ValueError(f"cannot split {r} rows over {n_steps} steps")


def _cast_io(jobs, step_of):
    in_specs, out_specs, out_shapes = [], [], []
    for job in jobs:
        _, r, c = job.src.shape
        in_specs.append(pl.BlockSpec(
            (None, job.rows, c), lambda *g, job=job: (job.layer, step_of(*g) // job.steps_per_block, 0)))
        out_specs.append(pl.BlockSpec(
            (job.rows, c), lambda *g, job=job: (step_of(*g) // job.steps_per_block, 0)))
        out_shapes.append(jax.ShapeDtypeStruct((r, c), BF16))
    return in_specs, out_specs, out_shapes


def _run_casts(src_refs, dst_refs):
    for src, dst in zip(src_refs, dst_refs):
        dst[...] = src[...].astype(dst.dtype)


def _ada_kernel(cond_ref, w_ref, b_ref, o_ref):
    s = _silu(cond_ref[...]).astype(BF16)
    o_ref[0] = _dot(s, w_ref[0].astype(BF16)) + b_ref[0]


def _ada_params(cond, ada_w, ada_b, tn=1024):
    depth, d, n6 = ada_w.shape
    rows = cond.shape[0]
    return pl.pallas_call(
        _ada_kernel,
        grid=(depth, n6 // tn),
        in_specs=[
            pl.BlockSpec((rows, d), lambda l, j: (0, 0)),
            pl.BlockSpec((1, d, tn), lambda l, j: (l, 0, j)),
            pl.BlockSpec((1, 1, tn), lambda l, j: (l, 0, j)),
        ],
        out_specs=pl.BlockSpec((1, rows, tn), lambda l, j: (l, 0, j)),
        out_shape=jax.ShapeDtypeStruct((depth, rows, n6), F32),
        compiler_params=_params("parallel", "parallel"),
        name="ada_params",
    )(cond, ada_w, ada_b.reshape(depth, 1, n6))


def _mod_matmul_kernel(*refs, n_cast):
    x_ref, sh_ref, sc_ref, w_ref, cs_ref = refs[:5]
    cast_src = refs[5:5 + n_cast]
    o_ref = refs[5 + n_cast]
    cast_dst = refs[6 + n_cast:6 + 2 * n_cast]
    a_ref = refs[6 + 2 * n_cast]

    @pl.when(pl.program_id(1) == 0)
    def _():
        a_ref[...] = (x_ref[...] * (1.0 + sc_ref[0]) + sh_ref[0]).astype(BF16)

    o_ref[...] = (_dot(a_ref[...], w_ref[...]) * cs_ref[...]).astype(o_ref.dtype)
    _run_casts(cast_src, cast_dst)


def _mod_matmul(x, sh, sc, w, col_scale, rows_per_batch, tm, tn, col_start=0, cast_srcs=(), name="mod_matmul"):
    m, d = x.shape
    n = w.shape[1] - col_start
    col_blk = col_start // tn
    tiles_per_batch = rows_per_batch // tm
    grid = (m // tm, n // tn)
    jobs = [_cast_job(src, layer, grid[0] * grid[1]) for src, layer in cast_srcs]
    cast_in, cast_out, cast_shapes = _cast_io(jobs, lambda i, j: i * grid[1] + j)
    if sh.shape[0] == 1:
        mod_idx = lambda i, j: (0, 0, 0)
    else:
        mod_idx = lambda i, j: (i // tiles_per_batch, 0, 0)
    return pl.pallas_call(
        functools.partial(_mod_matmul_kernel, n_cast=len(jobs)),
        grid=grid,
        in_specs=[
            pl.BlockSpec((tm, d), lambda i, j: (i, 0)),
            pl.BlockSpec((1, 1, d), mod_idx),
            pl.BlockSpec((1, 1, d), mod_idx),
            pl.BlockSpec((d, tn), lambda i, j: (0, col_blk + j)),
            pl.BlockSpec((1, tn), lambda i, j: (0, j)),
        ] + cast_in,
        out_specs=[pl.BlockSpec((tm, tn), lambda i, j: (i, j))] + cast_out,
        out_shape=[jax.ShapeDtypeStruct((m, n), BF16)] + cast_shapes,
        scratch_shapes=[pltpu.VMEM((tm, d), BF16)],
        compiler_params=_params("parallel", "arbitrary"),
        name=name,
    )(x, sh, sc, w, col_scale, *[job.src for job in jobs])


def _attn_bias_tables(rpb, rows):
    n_blocks = rows // Q_ROWS
    last_kb = rows - KEY_ROWS
    geoms = [(0, 0), (Q_ROWS, 0), (2 * Q_ROWS, 0),
             ((n_blocks - 2) * Q_ROWS, last_kb), ((n_blocks - 1) * Q_ROWS, last_kb)]
    n_heads, n_row_rel, _ = rpb.shape
    period = 2 * GRID_W + 1
    pad = jnp.zeros((n_heads, n_row_rel, period - (2 * WIN_W - 1)), rpb.dtype)
    base = jnp.concatenate([rpb[..., WIN_W - 1:], pad, rpb[..., :WIN_W - 1]], axis=-1)
    toep = jnp.tile(base, (1, 1, GRID_W))[..., :GRID_W * (period - 1)]
    toep = toep.reshape(n_heads, n_row_rel, GRID_W, period - 1)[..., :GRID_W]

    c = np.arange(GRID_W)[:, None]
    kc = np.arange(GRID_W)[None, :]
    c0 = np.clip(c - WIN_W // 2, 0, GRID_W - WIN_W)
    col_ok = (kc >= c0) & (kc < c0 + WIN_W)
    masked = jnp.full((n_heads, GRID_W, GRID_W), MASK_VALUE, rpb.dtype)
    tiles = jnp.where(col_ok[None, None], toep, MASK_VALUE)
    tables = []
    for rb, kb in geoms:
        q_rows = []
        for dr in range(Q_ROWS):
            r = rb + dr
            r0 = min(max(r - WIN_H // 2, 0), rows - WIN_H)
            blocks = [tiles[:, kb + i - r + WIN_H - 1] if r0 <= kb + i < r0 + WIN_H else masked
                      for i in range(KEY_ROWS)]
            q_rows.append(jnp.concatenate(blocks, axis=-1))
        tables.append(jnp.concatenate(q_rows, axis=1))
    return jnp.stack(tables)


def _attn_kernel(*refs, n_blocks, last_kb, n_cast):
    q_ref, k_ref, v_ref, kc_ref, vc_ref, bias_ref = refs[:6]
    o_ref = refs[6 + n_cast]
    s0, s1, sc0, sc1, p0, p1, pc0, pc1, den0, den1 = refs[7 + 2 * n_cast:]
    _run_casts(refs[6:6 + n_cast], refs[7 + n_cast:7 + 2 * n_cast])
    _attn_pipeline(q_ref, k_ref, v_ref, kc_ref, vc_ref, bias_ref, o_ref,
                   s0, s1, sc0, sc1, p0, p1, pc0, pc1, den0, den1, n_blocks=n_blocks, last_kb=last_kb)


def _attn_pipeline(q_ref, k_ref, v_ref, kc_ref, vc_ref, bias_ref, o_ref,
                   s0, s1, sc0, sc1, p0, p1, pc0, pc1, den0, den1, *, n_blocks, last_kb):
    s_bufs, sc_bufs, p_bufs, pc_bufs, den_bufs = (s0, s1), (sc0, sc1), (p0, p1), (pc0, pc1), (den0, den1)
    groups_per_item = n_blocks // ATTN_GROUP
    n_groups = q_ref.shape[0] * groups_per_item
    g_rows = ATTN_GROUP * Q_BLK
    contract_last = (((1,), (1,)), ((), ()))

    def key_start(t):
        kb = jnp.clip(Q_ROWS * t - WIN_H // 2, 0, last_kb)
        return pl.multiple_of(kb * GRID_W, GRID_W)

    def locate(g):
        item = g // groups_per_item
        local = g - item * groups_per_item
        return item, local, pl.multiple_of(local * g_rows, g_rows)

    def scores(g, slot):
        item, local, row0 = locate(g)
        q = q_ref[item, pl.ds(row0, g_rows), :]
        sc_bufs[slot][...] = lax.dot_general(q, kc_ref[item], contract_last, preferred_element_type=F32)
        for i in range(ATTN_GROUP):
            t = local * ATTN_GROUP + i
            geom = jnp.where(t < 2, t, jnp.where(t >= n_blocks - 2, t - (n_blocks - 5), 2))
            k = k_ref[item, pl.ds(key_start(t), K_BLK), :]
            s = lax.dot_general(q[i * Q_BLK:(i + 1) * Q_BLK], k, contract_last, preferred_element_type=F32)
            s_bufs[slot][i * Q_BLK:(i + 1) * Q_BLK, :] = s + bias_ref[geom, 0]

    def softmax(slot):
        s = s_bufs[slot][...]
        sc = sc_bufs[slot][...]
        m = jnp.maximum(jnp.max(s, axis=-1, keepdims=True), jnp.max(sc, axis=-1, keepdims=True))
        p = jnp.exp(s - m)
        pc = jnp.exp(sc - m)
        den_bufs[slot][...] = jnp.sum(p, axis=-1, keepdims=True) + jnp.sum(pc, axis=-1, keepdims=True)
        p_bufs[slot][...] = p.astype(BF16)
        pc_bufs[slot][...] = pc.astype(BF16)

    def weighted_sum(g, slot):
        item, local, row0 = locate(g)
        o_ctx = _dot(pc_bufs[slot][...], vc_ref[item])
        outs = []
        for i in range(ATTN_GROUP):
            v = v_ref[item, pl.ds(key_start(local * ATTN_GROUP + i), K_BLK), :]
            outs.append(_dot(p_bufs[slot][i * Q_BLK:(i + 1) * Q_BLK, :], v))
        o = (jnp.concatenate(outs, axis=0) + o_ctx) / den_bufs[slot][...]
        o_ref[item, pl.ds(row0, g_rows), :] = o.astype(o_ref.dtype)

    scores(0, 0)
    scores(1, 1)
    softmax(0)

    def steady(j, carry):
        for parity in range(2):
            tau = 2 * j + 2 + parity
            scores(tau, parity)
            softmax(1 - parity)
            weighted_sum(tau - 2, parity)
        return carry

    lax.fori_loop(0, (n_groups - 2) // 2, steady, 0)
    softmax(1)
    weighted_sum(n_groups - 2, 0)
    weighted_sum(n_groups - 1, 1)


def _neighbourhood_attention(qkv, kvc, bias, cast_srcs=()):
    bsz, n, _ = qkv.shape
    ctx_len = kvc.shape[1]
    rows = n // GRID_W
    n_blocks = rows // Q_ROWS
    assert n_blocks % (2 * ATTN_GROUP) == 0 and n_blocks // ATTN_GROUP >= 4
    assert bsz % ATTN_ITEMS == 0
    n_batch_steps = bsz // ATTN_ITEMS
    jobs = [_cast_job(src, layer, N_HEADS * n_batch_steps) for src, layer in cast_srcs]
    cast_in, cast_out, cast_shapes = _cast_io(jobs, lambda h, b: h * n_batch_steps + b)
    kern = functools.partial(_attn_kernel, n_blocks=n_blocks, last_kb=rows - KEY_ROWS, n_cast=len(jobs))
    seq_spec = lambda off: pl.BlockSpec((ATTN_ITEMS, n, HEAD_DIM), lambda h, b: (b, 0, off + h))
    ctx_spec = lambda off: pl.BlockSpec((ATTN_ITEMS, ctx_len, HEAD_DIM), lambda h, b: (b, 0, off + h))
    g_rows = ATTN_GROUP * Q_BLK
    slot_pair = lambda cols, dtype: [pltpu.VMEM((g_rows, cols), dtype)] * 2
    return pl.pallas_call(
        kern,
        grid=(N_HEADS, n_batch_steps),
        in_specs=[
            seq_spec(0), seq_spec(N_HEADS), seq_spec(2 * N_HEADS),
            ctx_spec(0), ctx_spec(N_HEADS),
            pl.BlockSpec((bias.shape[0], 1, Q_BLK, K_BLK), lambda h, b: (0, h, 0, 0)),
        ] + cast_in,
        out_specs=[pl.BlockSpec((ATTN_ITEMS, n, HEAD_DIM), lambda h, b: (b, 0, h))] + cast_out,
        out_shape=[jax.ShapeDtypeStruct((bsz, n, D_MODEL), BF16)] + cast_shapes,
        scratch_shapes=(slot_pair(K_BLK, F32) + slot_pair(ctx_len, F32) + slot_pair(K_BLK, BF16)
                        + slot_pair(ctx_len, BF16) + slot_pair(1, F32)),
        compiler_params=_params("parallel", "parallel"),
        name="neighbourhood_attention",
    )(qkv, qkv, qkv, kvc, kvc, bias, *[job.src for job in jobs])


def _proj_res_ln_kernel(y_ref, w_ref, h_ref, g_ref, lg_ref, lb_ref, o_ref):
    sub = y_ref.shape[0] // PROJ_SUB_BLOCKS
    for r in range(PROJ_SUB_BLOCKS):
        rows = slice(r * sub, (r + 1) * sub)
        y = _dot(y_ref[rows, :], w_ref[...])
        o_ref[rows, :] = _layer_norm(ALPHA * h_ref[rows, :] + g_ref[0] * y, lg_ref[...], lb_ref[...])


def _proj_res_ln(y, w, h, gate, ln_g, ln_b, rows_per_batch, tm=1024):
    m, k = y.shape
    d = w.shape[1]
    tiles_per_batch = rows_per_batch // tm
    return pl.pallas_call(
        _proj_res_ln_kernel,
        grid=(m // tm,),
        in_specs=[
            pl.BlockSpec((tm, k), lambda i: (i, 0)),
            pl.BlockSpec((k, d), lambda i: (0, 0), pipeline_mode=pl.Buffered(1)),
            pl.BlockSpec((tm, d), lambda i: (i, 0)),
            pl.BlockSpec((1, 1, d), lambda i: (i // tiles_per_batch, 0, 0)),
            pl.BlockSpec((1, d), lambda i: (0, 0)),
            pl.BlockSpec((1, d), lambda i: (0, 0)),
        ],
        out_specs=pl.BlockSpec((tm, d), lambda i: (i, 0)),
        out_shape=jax.ShapeDtypeStruct((m, d), F32),
        compiler_params=_params("parallel"),
        name="proj_res_ln",
    )(y, w, h, gate, ln_g.reshape(1, d), ln_b.reshape(1, d))


def _ffn_kernel(h_ref, sh_ref, sc_ref, g_ref, wg_ref, wu_ref, wo_ref, lg_ref, lb_ref, o_ref, a_ref):
    f = pl.program_id(1)

    @pl.when(f == 0)
    def _():
        h = h_ref[...]
        a_ref[...] = (h * (1.0 + sc_ref[0]) + sh_ref[0]).astype(BF16)
        o_ref[...] = ALPHA * h

    a = a_ref[...]
    act = _silu(_dot(a, wg_ref[...])) * _dot(a, wu_ref[...])
    o_ref[...] += g_ref[0] * _dot(act.astype(BF16), wo_ref[...])

    @pl.when(f == pl.num_programs(1) - 1)
    def _():
        o_ref[...] = _layer_norm(o_ref[...], lg_ref[...], lb_ref[...])


def _ffn(h, sh, sc, gate, w_in, w_out, ln_g, ln_b, rows_per_batch, tm=1024, tf=512):
    m, d = h.shape
    d_ff = w_out.shape[0]
    n_f = d_ff // tf
    tiles_per_batch = rows_per_batch // tm
    mod_spec = pl.BlockSpec((1, 1, d), lambda i, f: (i // tiles_per_batch, 0, 0))
    vec_spec = pl.BlockSpec((1, d), lambda i, f: (0, 0))
    return pl.pallas_call(
        _ffn_kernel,
        grid=(m // tm, n_f),
        in_specs=[
            pl.BlockSpec((tm, d), lambda i, f: (i, 0)),
            mod_spec, mod_spec, mod_spec,
            pl.BlockSpec((d, tf), lambda i, f: (0, f)),
            pl.BlockSpec((d, tf), lambda i, f: (0, n_f + f)),
            pl.BlockSpec((tf, d), lambda i, f: (f, 0)),
            vec_spec, vec_spec,
        ],
        out_specs=pl.BlockSpec((tm, d), lambda i, f: (i, 0)),
        out_shape=jax.ShapeDtypeStruct((m, d), F32),
        scratch_shapes=[pltpu.VMEM((tm, d), BF16)],
        compiler_params=_params("parallel", "arbitrary"),
        name="swiglu_ffn",
    )(h, sh, sc, gate, w_in, w_in, w_out, ln_g.reshape(1, d), ln_b.reshape(1, d))


def _gmlp_kernel(*refs, n_cast):
    h_ref, sh_ref, sc_ref, wu_ref, wv_ref, lg_ref, lb_ref, ws_ref, bs_ref = refs[:9]
    o_ref = refs[9 + n_cast]
    u_ref, v_ref = refs[10 + 2 * n_cast:]
    _run_casts(refs[9:9 + n_cast], refs[10 + n_cast:10 + 2 * n_cast])
    a = (h_ref[...] * (1.0 + sc_ref[0]) + sh_ref[0]).astype(BF16)

    def gelu(z):
        return 0.5 * z * (1.0 + lax.erf(z * (2.0 ** -0.5)))

    d = u_ref.shape[1]
    col_chunks = [slice(c * GMLP_COL_CHUNK, (c + 1) * GMLP_COL_CHUNK) for c in range(d // GMLP_COL_CHUNK)]
    v = jnp.concatenate([gelu(_dot(a, wv_ref[:, cols])) for cols in col_chunks], axis=-1)
    v_ref[...] = _layer_norm(v, lg_ref[...], lb_ref[...]).astype(BF16)
    for cols in col_chunks:
        u_ref[:, cols] = gelu(_dot(a, wu_ref[:, cols]))
    for ci in range(h_ref.shape[0] // CHUNK):
        rows = slice(ci * CHUNK, (ci + 1) * CHUNK)
        for g in range(N_GROUPS):
            cols = slice(g * GROUP_DIM, (g + 1) * GROUP_DIM)
            mixed = _dot(ws_ref[g], v_ref[rows, cols]) + bs_ref[:, cols]
            o_ref[rows, cols] = (u_ref[rows, cols] * mixed).astype(o_ref.dtype)


def _gmlp_gate(h, sh, sc, w_in, ln_g, ln_b, w_s, b_s_full, rows_per_batch, tm=256, cast_srcs=()):
    m, d = h.shape
    tiles_per_batch = rows_per_batch // tm
    mod_spec = pl.BlockSpec((1, 1, d), lambda i: (i // tiles_per_batch, 0, 0))
    vec_spec = pl.BlockSpec((1, d), lambda i: (0, 0))
    jobs = [_cast_job(src, layer, m // tm) for src, layer in cast_srcs]
    cast_in, cast_out, cast_shapes = _cast_io(jobs, lambda i: i)
    return pl.pallas_call(
        functools.partial(_gmlp_kernel, n_cast=len(jobs)),
        grid=(m // tm,),
        in_specs=[
            pl.BlockSpec((tm, d), lambda i: (i, 0)),
            mod_spec, mod_spec,
            pl.BlockSpec((d, d), lambda i: (0, 0)),
            pl.BlockSpec((d, d), lambda i: (0, 1)),
            vec_spec, vec_spec,
            pl.BlockSpec((N_GROUPS, CHUNK, CHUNK), lambda i: (0, 0, 0)),
            pl.BlockSpec((CHUNK, d), lambda i: (0, 0)),
        ] + cast_in,
        out_specs=[pl.BlockSpec((tm, d), lambda i: (i, 0))] + cast_out,
        out_shape=[jax.ShapeDtypeStruct((m, d), BF16)] + cast_shapes,
        scratch_shapes=[pltpu.VMEM((tm, d), F32), pltpu.VMEM((tm, d), BF16)],
        compiler_params=_params("arbitrary"),
        name="gmlp_gate",
    )(h, sh, sc, w_in, w_in, ln_g.reshape(1, d), ln_b.reshape(1, d), w_s, b_s_full, *[job.src for job in jobs])


def kernel(x, c, ctx, c_ctx, ada_w, ada_b, ln_g, ln_b, na_w_qkv, na_w_o, na_rpb, gm_w_in, gm_ln_g, gm_ln_b,
           gm_w_s, gm_b_s, gm_w_out, ffn_w_in, ffn_w_out):
    bsz, n, d = x.shape
    ctx_len = ctx.shape[1]
    rows = n // GRID_W
    assert d == D_MODEL and n % (GRID_W * Q_ROWS) == 0 and rows >= KEY_ROWS + 2 * Q_ROWS

    cond = jnp.concatenate([c, c_ctx[None, :], jnp.zeros((8 - bsz - 1, d), F32)], axis=0)
    ada = _ada_params(cond, ada_w, ada_b)

    def mod_vectors(layer):
        parts = jnp.split(ada[layer], 6, axis=-1)
        latent = [p[:bsz].reshape(bsz, 1, d) for p in parts]
        context = [p[bsz:bsz + 1].reshape(1, 1, d) for p in parts]
        return latent, context

    h = x.reshape(bsz * n, d)

    (sh1, sc1, g1, sh2, sc2, g2), (csh1, csc1, _, _, _, _) = mod_vectors(0)
    w_qkv = na_w_qkv[0].astype(BF16)
    q_scale = jnp.concatenate([jnp.full((1, d), HEAD_DIM ** -0.5, F32), jnp.ones((1, 2 * d), F32)], axis=1)
    qkv, = _mod_matmul(h, sh1, sc1, w_qkv, q_scale, n, tm=1024, tn=1024, name="qkv_proj")
    kvc, = _mod_matmul(ctx.reshape(bsz * ctx_len, d), csh1, csc1, w_qkv, jnp.ones((1, 2 * d), F32), ctx_len,
                       tm=bsz * ctx_len, tn=1024, col_start=d, name="ctx_kv_proj")
    bias = _attn_bias_tables(na_rpb[0], rows)
    o, w_o, gm_w_in_bf, gm_w_out_bf, ffn_w_in0, ffn_w_out0 = _neighbourhood_attention(
        qkv.reshape(bsz, n, 3 * d), kvc.reshape(bsz, ctx_len, 2 * d), bias,
        cast_srcs=[(na_w_o, 0), (gm_w_in, 0), (gm_w_out, 0), (ffn_w_in, 0), (ffn_w_out, 0)])
    h = _proj_res_ln(o.reshape(bsz * n, d), w_o, h, g1, ln_g[0, 0], ln_b[0, 0], n)
    h = _ffn(h, sh2, sc2, g2, ffn_w_in0, ffn_w_out0, ln_g[0, 1], ln_b[0, 1], n)

    (sh1, sc1, g1, sh2, sc2, g2), _ = mod_vectors(1)
    b_s_full = jnp.repeat(jnp.transpose(gm_b_s[0]), GROUP_DIM, axis=1)
    uv, ffn_w_in1, ffn_w_out1 = _gmlp_gate(h, sh1, sc1, gm_w_in_bf, gm_ln_g[0], gm_ln_b[0], gm_w_s[0].astype(BF16),
                                            b_s_full, n, cast_srcs=[(ffn_w_in, 1), (ffn_w_out, 1)])
    h = _proj_res_ln(uv, gm_w_out_bf, h, g1, ln_g[1, 0], ln_b[1, 0], n)
    h = _ffn(h, sh2, sc2, g2, ffn_w_in1, ffn_w_out1, ln_g[1, 1], ln_b[1, 1], n)
    return h.reshape(bsz, n, d)
```

```python
import functools
from typing import NamedTuple

import numpy as np
import jax
import jax.numpy as jnp
from jax import lax
from jax.experimental import pallas as pl
from jax.experimental.pallas import tpu as pltpu

D_MODEL = 2048
GRID_W = 64
N_HEADS = 16
HEAD_DIM = D_MODEL // N_HEADS
WIN_H = 8
WIN_W = 16
CHUNK = 128
N_GROUPS = 16
GROUP_DIM = D_MODEL // N_GROUPS
DEPTH = 2
ALPHA = (2 * DEPTH) ** 0.25
LN_EPS = 1e-5

VMEM_LIMIT_BYTES = 60 * 1024 * 1024
MASK_VALUE = -1e30
LOG2_E = 1.4426950408889634

Q_ROWS = 2
KEY_ROWS = Q_ROWS + WIN_H - 1
Q_BLK = Q_ROWS * GRID_W
K_BLK = KEY_ROWS * GRID_W
GMLP_COL_CHUNK = 512
GMLP_SUB_ROWS = 256
PROJ_SUB_BLOCKS = 4
ATTN_GROUP = 4
ATTN_ITEMS = 2
STRIP_PAD_TILES = 1

BF16 = jnp.bfloat16
F32 = jnp.float32
BF16_SUBLANES = 16


def _params(*sem):
    return pltpu.CompilerParams(dimension_semantics=sem, vmem_limit_bytes=VMEM_LIMIT_BYTES)


def _layer_norm(x, g, b):
    mu = jnp.mean(x, axis=-1, keepdims=True)
    xc = x - mu
    var = jnp.mean(xc * xc, axis=-1, keepdims=True)
    return xc * lax.rsqrt(var + LN_EPS) * g + b


def _silu(x):
    return x * (1.0 / (1.0 + jnp.exp(-x)))


def _dot(a, b):
    return jnp.dot(a, b, preferred_element_type=F32)


class _CastJob(NamedTuple):
    src: jax.Array
    layer: int
    rows: int
    steps_per_block: int


def _cast_job(src, layer, n_steps):
    r = src.shape[1]
    for steps_per_block in (1, 2, 4, 8):
        rows, rem = divmod(r * steps_per_block, n_steps)
        if rem == 0 and rows % BF16_SUBLANES == 0:
            return _CastJob(src, layer, rows, steps_per_block)
    raise ValueError(f"cannot split {r} rows over {n_steps} steps")


def _cast_io(jobs, step_of):
    in_specs, out_specs, out_shapes = [], [], []
    for job in jobs:
        _, r, c = job.src.shape
        in_specs.append(pl.BlockSpec(
            (None, job.rows, c), lambda *g, job=job: (job.layer, step_of(*g) // job.steps_per_block, 0)))
        out_specs.append(pl.BlockSpec(
            (job.rows, c), lambda *g, job=job: (step_of(*g) // job.steps_per_block, 0)))
        out_shapes.append(jax.ShapeDtypeStruct((r, c), BF16))
    return in_specs, out_specs, out_shapes


def _run_casts(src_refs, dst_refs):
    for src, dst in zip(src_refs, dst_refs):
        dst[...] = src[...].astype(dst.dtype)


def _ada_kernel(cond_ref, w_ref, b_ref, o_ref):
    s = _silu(cond_ref[...]).astype(BF16)
    o_ref[0] = _dot(s, w_ref[0].astype(BF16)) + b_ref[0]


def _ada_params(cond, ada_w, ada_b, tn=1024):
    depth, d, n6 = ada_w.shape
    rows = cond.shape[0]
    return pl.pallas_call(
        _ada_kernel,
        grid=(depth, n6 // tn),
        in_specs=[
            pl.BlockSpec((rows, d), lambda l, j: (0, 0)),
            pl.BlockSpec((1, d, tn), lambda l, j: (l, 0, j)),
            pl.BlockSpec((1, 1, tn), lambda l, j: (l, 0, j)),
        ],
        out_specs=pl.BlockSpec((1, rows, tn), lambda l, j: (l, 0, j)),
        out_shape=jax.ShapeDtypeStruct((depth, rows, n6), F32),
        compiler_params=_params("parallel", "parallel"),
        name="ada_params",
    )(cond, ada_w, ada_b.reshape(depth, 1, n6))


def _mod_matmul_kernel(*refs, n_cast):
    x_ref, sh_ref, sc_ref, w_ref, cs_ref = refs[:5]
    cast_src = refs[5:5 + n_cast]
    o_ref = refs[5 + n_cast]
    cast_dst = refs[6 + n_cast:6 + 2 * n_cast]
    a_ref = refs[6 + 2 * n_cast]

    @pl.when(pl.program_id(1) == 0)
    def _():
        a_ref[...] = (x_ref[...] * (1.0 + sc_ref[0]) + sh_ref[0]).astype(BF16)

    o_ref[...] = (_dot(a_ref[...], w_ref[...]) * cs_ref[...]).astype(o_ref.dtype)
    _run_casts(cast_src, cast_dst)


def _mod_matmul(x, sh, sc, w, col_scale, rows_per_batch, tm, tn, col_start=0, cast_srcs=(), name="mod_matmul"):
    m, d = x.shape
    n = w.shape[1] - col_start
    col_blk = col_start // tn
    tiles_per_batch = rows_per_batch // tm
    grid = (m // tm, n // tn)
    jobs = [_cast_job(src, layer, grid[0] * grid[1]) for src, layer in cast_srcs]
    cast_in, cast_out, cast_shapes = _cast_io(jobs, lambda i, j: i * grid[1] + j)
    if sh.shape[0] == 1:
        mod_idx = lambda i, j: (0, 0, 0)
    else:
        mod_idx = lambda i, j: (i // tiles_per_batch, 0, 0)
    return pl.pallas_call(
        functools.partial(_mod_matmul_kernel, n_cast=len(jobs)),
        grid=grid,
        in_specs=[
            pl.BlockSpec((tm, d), lambda i, j: (i, 0)),
            pl.BlockSpec((1, 1, d), mod_idx),
            pl.BlockSpec((1, 1, d), mod_idx),
            pl.BlockSpec((d, tn), lambda i, j: (0, col_blk + j)),
            pl.BlockSpec((1, tn), lambda i, j: (0, j)),
        ] + cast_in,
        out_specs=[pl.BlockSpec((tm, tn), lambda i, j: (i, j))] + cast_out,
        out_shape=[jax.ShapeDtypeStruct((m, n), BF16)] + cast_shapes,
        scratch_shapes=[pltpu.VMEM((tm, d), BF16)],
        compiler_params=_params("parallel", "arbitrary"),
        name=name,
    )(x, sh, sc, w, col_scale, *[job.src for job in jobs])


def _attn_geometries(rows):
    n_blocks = rows // Q_ROWS
    last_kb = rows - KEY_ROWS
    geoms = [(0, 0), (Q_ROWS, 0), (2 * Q_ROWS, 0),
             ((n_blocks - 2) * Q_ROWS, last_kb), ((n_blocks - 1) * Q_ROWS, last_kb)]
    layout = []
    for rb, kb in geoms:
        per_row = []
        for dr in range(Q_ROWS):
            r = rb + dr
            r0 = min(max(r - WIN_H // 2, 0), rows - WIN_H)
            first_tile = kb - r + (WIN_H - 1) + STRIP_PAD_TILES
            assert 0 <= first_tile and first_tile + KEY_ROWS <= 2 * WIN_H - 1 + 2 * STRIP_PAD_TILES
            per_row.append((first_tile, max(r0 - kb, 0), min(r0 - kb + WIN_H, KEY_ROWS)))
        layout.append(per_row)
    return layout


def _attn_bias_strip(rpb):
    n_heads, n_row_rel, _ = rpb.shape
    period = 2 * GRID_W + 1
    pad = jnp.zeros((n_heads, n_row_rel, period - (2 * WIN_W - 1)), rpb.dtype)
    base = jnp.concatenate([rpb[..., WIN_W - 1:], pad, rpb[..., :WIN_W - 1]], axis=-1) * LOG2_E
    toep = jnp.tile(base, (1, 1, GRID_W))[..., :GRID_W * (period - 1)]
    toep = toep.reshape(n_heads, n_row_rel, GRID_W, period - 1)[..., :GRID_W]

    c = np.arange(GRID_W)[:, None]
    kc = np.arange(GRID_W)[None, :]
    c0 = np.clip(c - WIN_W // 2, 0, GRID_W - WIN_W)
    col_ok = (kc >= c0) & (kc < c0 + WIN_W)
    tiles = jnp.where(col_ok[None, None], toep, MASK_VALUE)
    masked = jnp.full((n_heads, STRIP_PAD_TILES, GRID_W, GRID_W), MASK_VALUE, rpb.dtype)
    strip = jnp.concatenate([masked, tiles, masked], axis=1)
    return jnp.transpose(strip, (0, 2, 1, 3)).reshape(n_heads, GRID_W, -1)


def _attn_kernel(*refs, n_blocks, last_kb, n_cast, layout):
    q_ref, k_ref, v_ref, kc_ref, vc_ref, strip_ref = refs[:6]
    o_ref = refs[6 + n_cast]
    s0, s1, sc0, sc1, p0, p1, pc0, pc1, den0, den1, bias_ref = refs[7 + 2 * n_cast:]
    _run_casts(refs[6:6 + n_cast], refs[7 + n_cast:7 + 2 * n_cast])

    @pl.when(pl.program_id(1) == 0)
    def _():
        key_row = lax.broadcasted_iota(jnp.int32, (GRID_W, K_BLK), 1) // GRID_W
        for geom, per_row in enumerate(layout):
            for dr, (first_tile, lo, hi) in enumerate(per_row):
                window = strip_ref[0, :, first_tile * GRID_W:first_tile * GRID_W + K_BLK]
                valid = (key_row >= lo) & (key_row < hi)
                bias_ref[geom, dr * GRID_W:(dr + 1) * GRID_W, :] = jnp.where(valid, window, MASK_VALUE)

    _attn_pipeline(q_ref, k_ref, v_ref, kc_ref, vc_ref, bias_ref, o_ref,
                   s0, s1, sc0, sc1, p0, p1, pc0, pc1, den0, den1, n_blocks=n_blocks, last_kb=last_kb)


def _attn_pipeline(q_ref, k_ref, v_ref, kc_ref, vc_ref, bias_ref, o_ref,
                   s0, s1, sc0, sc1, p0, p1, pc0, pc1, den0, den1, *, n_blocks, last_kb):
    s_bufs, sc_bufs, p_bufs, pc_bufs, den_bufs = (s0, s1), (sc0, sc1), (p0, p1), (pc0, pc1), (den0, den1)
    groups_per_item = n_blocks // ATTN_GROUP
    n_groups = q_ref.shape[0] * groups_per_item
    g_rows = ATTN_GROUP * Q_BLK
    contract_last = (((1,), (1,)), ((), ()))

    def key_start(t):
        kb = jnp.clip(Q_ROWS * t - WIN_H // 2, 0, last_kb)
        return pl.multiple_of(kb * GRID_W, GRID_W)

    def locate(g):
        item = g // groups_per_item
        local = g - item * groups_per_item
        return item, local, pl.multiple_of(local * g_rows, g_rows)

    def scores(g, slot):
        item, local, row0 = locate(g)
        q = q_ref[item, pl.ds(row0, g_rows), :]
        sc_bufs[slot][...] = lax.dot_general(q, kc_ref[item], contract_last, preferred_element_type=F32)
        for i in range(ATTN_GROUP):
            t = local * ATTN_GROUP + i
            geom = jnp.where(t < 2, t, jnp.where(t >= n_blocks - 2, t - (n_blocks - 5), 2))
            k = k_ref[item, pl.ds(key_start(t), K_BLK), :]
            s = lax.dot_general(q[i * Q_BLK:(i + 1) * Q_BLK], k, contract_last, preferred_element_type=F32)
            s_bufs[slot][i * Q_BLK:(i + 1) * Q_BLK, :] = s + bias_ref[geom]

    def softmax(slot):
        s = s_bufs[slot][...]
        sc = sc_bufs[slot][...]
        m = jnp.maximum(jnp.max(s, axis=-1, keepdims=True), jnp.max(sc, axis=-1, keepdims=True))
        p = jnp.exp2(s - m)
        pc = jnp.exp2(sc - m)
        den_bufs[slot][...] = jnp.sum(p, axis=-1, keepdims=True) + jnp.sum(pc, axis=-1, keepdims=True)
        p_bufs[slot][...] = p.astype(BF16)
        pc_bufs[slot][...] = pc.astype(BF16)

    def weighted_sum(g, slot):
        item, local, row0 = locate(g)
        o_ctx = _dot(pc_bufs[slot][...], vc_ref[item])
        outs = []
        for i in range(ATTN_GROUP):
            v = v_ref[item, pl.ds(key_start(local * ATTN_GROUP + i), K_BLK), :]
            outs.append(_dot(p_bufs[slot][i * Q_BLK:(i + 1) * Q_BLK, :], v))
        o = (jnp.concatenate(outs, axis=0) + o_ctx) / den_bufs[slot][...]
        o_ref[item, pl.ds(row0, g_rows), :] = o.astype(o_ref.dtype)

    scores(0, 0)
    scores(1, 1)
    softmax(0)

    def steady(j, carry):
        for parity in range(2):
            tau = 2 * j + 2 + parity
            scores(tau, parity)
            softmax(1 - parity)
            weighted_sum(tau - 2, parity)
        return carry

    lax.fori_loop(0, (n_groups - 2) // 2, steady, 0)
    softmax(1)
    weighted_sum(n_groups - 2, 0)
    weighted_sum(n_groups - 1, 1)


def _neighbourhood_attention(qkv, kvc, strip, cast_srcs=()):
    bsz, n, _ = qkv.shape
    ctx_len = kvc.shape[1]
    rows = n // GRID_W
    n_blocks = rows // Q_ROWS
    assert n_blocks % (2 * ATTN_GROUP) == 0 and n_blocks // ATTN_GROUP >= 4
    assert bsz % ATTN_ITEMS == 0
    n_batch_steps = bsz // ATTN_ITEMS
    jobs = [_cast_job(src, layer, N_HEADS * n_batch_steps) for src, layer in cast_srcs]
    cast_in, cast_out, cast_shapes = _cast_io(jobs, lambda h, b: h * n_batch_steps + b)
    layout = _attn_geometries(rows)
    kern = functools.partial(_attn_kernel, n_blocks=n_blocks, last_kb=rows - KEY_ROWS, n_cast=len(jobs), layout=layout)
    seq_spec = lambda off: pl.BlockSpec((ATTN_ITEMS, n, HEAD_DIM), lambda h, b: (b, 0, off + h))
    ctx_spec = lambda off: pl.BlockSpec((ATTN_ITEMS, ctx_len, HEAD_DIM), lambda h, b: (b, 0, off + h))
    g_rows = ATTN_GROUP * Q_BLK
    slot_pair = lambda cols, dtype: [pltpu.VMEM((g_rows, cols), dtype)] * 2
    return pl.pallas_call(
        kern,
        grid=(N_HEADS, n_batch_steps),
        in_specs=[
            seq_spec(0), seq_spec(N_HEADS), seq_spec(2 * N_HEADS),
            ctx_spec(0), ctx_spec(N_HEADS),
            pl.BlockSpec((1,) + strip.shape[1:], lambda h, b: (h, 0, 0)),
        ] + cast_in,
        out_specs=[pl.BlockSpec((ATTN_ITEMS, n, HEAD_DIM), lambda h, b: (b, 0, h))] + cast_out,
        out_shape=[jax.ShapeDtypeStruct((bsz, n, D_MODEL), BF16)] + cast_shapes,
        scratch_shapes=(slot_pair(K_BLK, F32) + slot_pair(ctx_len, F32) + slot_pair(K_BLK, BF16)
                        + slot_pair(ctx_len, BF16) + slot_pair(1, F32)
                        + [pltpu.VMEM((len(layout), Q_BLK, K_BLK), F32)]),
        compiler_params=_params("arbitrary", "arbitrary"),
        name="neighbourhood_attention",
    )(qkv, qkv, qkv, kvc, kvc, strip, *[job.src for job in jobs])


def _proj_res_ln_kernel(y_ref, w_ref, h_ref, g_ref, lg_ref, lb_ref, o_ref):
    sub = y_ref.shape[0] // PROJ_SUB_BLOCKS
    for r in range(PROJ_SUB_BLOCKS):
        rows = slice(r * sub, (r + 1) * sub)
        y = _dot(y_ref[rows, :], w_ref[...])
        o_ref[rows, :] = _layer_norm(ALPHA * h_ref[rows, :] + g_ref[0] * y, lg_ref[...], lb_ref[...])


def _proj_res_ln(y, w, h, gate, ln_g, ln_b, rows_per_batch, tm=1024):
    m, k = y.shape
    d = w.shape[1]
    tiles_per_batch = rows_per_batch // tm
    return pl.pallas_call(
        _proj_res_ln_kernel,
        grid=(m // tm,),
        in_specs=[
            pl.BlockSpec((tm, k), lambda i: (i, 0)),
            pl.BlockSpec((k, d), lambda i: (0, 0), pipeline_mode=pl.Buffered(1)),
            pl.BlockSpec((tm, d), lambda i: (i, 0)),
            pl.BlockSpec((1, 1, d), lambda i: (i // tiles_per_batch, 0, 0)),
            pl.BlockSpec((1, d), lambda i: (0, 0)),
            pl.BlockSpec((1, d), lambda i: (0, 0)),
        ],
        out_specs=pl.BlockSpec((tm, d), lambda i: (i, 0)),
        out_shape=jax.ShapeDtypeStruct((m, d), F32),
        compiler_params=_params("parallel"),
        name="proj_res_ln",
    )(y, w, h, gate, ln_g.reshape(1, d), ln_b.reshape(1, d))


def _ffn_kernel(h_ref, sh_ref, sc_ref, g_ref, wg_ref, wu_ref, wo_ref, lg_ref, lb_ref, o_ref, a_ref):
    f = pl.program_id(1)

    @pl.when(f == 0)
    def _():
        h = h_ref[...]
        a_ref[...] = (h * (1.0 + sc_ref[0]) + sh_ref[0]).astype(BF16)
        o_ref[...] = ALPHA * h

    a = a_ref[...]
    act = _silu(_dot(a, wg_ref[...])) * _dot(a, wu_ref[...])
    o_ref[...] += g_ref[0] * _dot(act.astype(BF16), wo_ref[...])

    @pl.when(f == pl.num_programs(1) - 1)
    def _():
        o_ref[...] = _layer_norm(o_ref[...], lg_ref[...], lb_ref[...])


def _ffn(h, sh, sc, gate, w_in, w_out, ln_g, ln_b, rows_per_batch, tm=1024, tf=512):
    m, d = h.shape
    d_ff = w_out.shape[0]
    n_f = d_ff // tf
    tiles_per_batch = rows_per_batch // tm
    mod_spec = pl.BlockSpec((1, 1, d), lambda i, f: (i // tiles_per_batch, 0, 0))
    vec_spec = pl.BlockSpec((1, d), lambda i, f: (0, 0))
    return pl.pallas_call(
        _ffn_kernel,
        grid=(m // tm, n_f),
        in_specs=[
            pl.BlockSpec((tm, d), lambda i, f: (i, 0)),
            mod_spec, mod_spec, mod_spec,
            pl.BlockSpec((d, tf), lambda i, f: (0, f)),
            pl.BlockSpec((d, tf), lambda i, f: (0, n_f + f)),
            pl.BlockSpec((tf, d), lambda i, f: (f, 0)),
            vec_spec, vec_spec,
        ],
        out_specs=pl.BlockSpec((tm, d), lambda i, f: (i, 0)),
        out_shape=jax.ShapeDtypeStruct((m, d), F32),
        scratch_shapes=[pltpu.VMEM((tm, d), BF16)],
        compiler_params=_params("parallel", "arbitrary"),
        name="swiglu_ffn",
    )(h, sh, sc, gate, w_in, w_in, w_out, ln_g.reshape(1, d), ln_b.reshape(1, d))


def _gmlp_kernel(*refs, n_cast):
    h_ref, sh_ref, sc_ref, wu_ref, wv_ref, lg_ref, lb_ref, ws_ref, bs_ref = refs[:9]
    o_ref = refs[9 + n_cast]
    u_ref, v_ref = refs[10 + 2 * n_cast:]
    _run_casts(refs[9:9 + n_cast], refs[10 + n_cast:10 + 2 * n_cast])

    def gelu(z):
        return 0.5 * z * (1.0 + lax.erf(z * (2.0 ** -0.5)))

    d = u_ref.shape[1]
    col_chunks = [slice(c * GMLP_COL_CHUNK, (c + 1) * GMLP_COL_CHUNK) for c in range(d // GMLP_COL_CHUNK)]
    for r0 in range(0, h_ref.shape[0], GMLP_SUB_ROWS):
        sub = slice(r0, r0 + GMLP_SUB_ROWS)
        a = (h_ref[sub, :] * (1.0 + sc_ref[0]) + sh_ref[0]).astype(BF16)
        v = jnp.concatenate([gelu(_dot(a, wv_ref[:, cols])) for cols in col_chunks], axis=-1)
        v_ref[sub, :] = _layer_norm(v, lg_ref[...], lb_ref[...]).astype(BF16)
        for cols in col_chunks:
            u_ref[sub, cols] = gelu(_dot(a, wu_ref[:, cols]))
        for c0 in range(r0, r0 + GMLP_SUB_ROWS, CHUNK):
            rows = slice(c0, c0 + CHUNK)
            for g in range(N_GROUPS):
                cols = slice(g * GROUP_DIM, (g + 1) * GROUP_DIM)
                mixed = _dot(ws_ref[g], v_ref[rows, cols]) + bs_ref[:, cols]
                o_ref[rows, cols] = (u_ref[rows, cols] * mixed).astype(o_ref.dtype)


def _gmlp_gate(h, sh, sc, w_in, ln_g, ln_b, w_s, b_s_full, rows_per_batch, tm=512, cast_srcs=()):
    m, d = h.shape
    tiles_per_batch = rows_per_batch // tm
    mod_spec = pl.BlockSpec((1, 1, d), lambda i: (i // tiles_per_batch, 0, 0))
    vec_spec = pl.BlockSpec((1, d), lambda i: (0, 0))
    jobs = [_cast_job(src, layer, m // tm) for src, layer in cast_srcs]
    cast_in, cast_out, cast_shapes = _cast_io(jobs, lambda i: i)
    return pl.pallas_call(
        functools.partial(_gmlp_kernel, n_cast=len(jobs)),
        grid=(m // tm,),
        in_specs=[
            pl.BlockSpec((tm, d), lambda i: (i, 0)),
            mod_spec, mod_spec,
            pl.BlockSpec((d, d), lambda i: (0, 0)),
            pl.BlockSpec((d, d), lambda i: (0, 1)),
            vec_spec, vec_spec,
            pl.BlockSpec((N_GROUPS, CHUNK, CHUNK), lambda i: (0, 0, 0)),
            pl.BlockSpec((CHUNK, d), lambda i: (0, 0)),
        ] + cast_in,
        out_specs=[pl.BlockSpec((tm, d), lambda i: (i, 0))] + cast_out,
        out_shape=[jax.ShapeDtypeStruct((m, d), BF16)] + cast_shapes,
        scratch_shapes=[pltpu.VMEM((tm, d), F32), pltpu.VMEM((tm, d), BF16)],
        compiler_params=_params("arbitrary"),
        name="gmlp_gate",
    )(h, sh, sc, w_in, w_in, ln_g.reshape(1, d), ln_b.reshape(1, d), w_s, b_s_full, *[job.src for job in jobs])


def kernel(x, c, ctx, c_ctx, ada_w, ada_b, ln_g, ln_b, na_w_qkv, na_w_o, na_rpb, gm_w_in, gm_ln_g, gm_ln_b,
           gm_w_s, gm_b_s, gm_w_out, ffn_w_in, ffn_w_out):
    bsz, n, d = x.shape
    ctx_len = ctx.shape[1]
    rows = n // GRID_W
    assert d == D_MODEL and n % (GRID_W * Q_ROWS) == 0 and rows >= KEY_ROWS + 2 * Q_ROWS

    cond = jnp.concatenate([c, c_ctx[None, :], jnp.zeros((8 - bsz - 1, d), F32)], axis=0)
    ada = _ada_params(cond, ada_w, ada_b)

    def mod_vectors(layer):
        parts = jnp.split(ada[layer], 6, axis=-1)
        latent = [p[:bsz].reshape(bsz, 1, d) for p in parts]
        context = [p[bsz:bsz + 1].reshape(1, 1, d) for p in parts]
        return latent, context

    h = x.reshape(bsz * n, d)

    (sh1, sc1, g1, sh2, sc2, g2), (csh1, csc1, _, _, _, _) = mod_vectors(0)
    w_qkv = na_w_qkv[0].astype(BF16)
    q_scale = jnp.concatenate([jnp.full((1, d), HEAD_DIM ** -0.5 * LOG2_E, F32), jnp.ones((1, 2 * d), F32)], axis=1)
    qkv, = _mod_matmul(h, sh1, sc1, w_qkv, q_scale, n, tm=1024, tn=1024, name="qkv_proj")
    kvc, = _mod_matmul(ctx.reshape(bsz * ctx_len, d), csh1, csc1, w_qkv, jnp.ones((1, 2 * d), F32), ctx_len,
                       tm=bsz * ctx_len, tn=1024, col_start=d, name="ctx_kv_proj")
    strip = _attn_bias_strip(na_rpb[0])
    o, w_o, gm_w_in_bf, gm_w_out_bf, ffn_w_in0, ffn_w_out0 = _neighbourhood_attention(
        qkv.reshape(bsz, n, 3 * d), kvc.reshape(bsz, ctx_len, 2 * d), strip,
        cast_srcs=[(na_w_o, 0), (gm_w_in, 0), (gm_w_out, 0), (ffn_w_in, 0), (ffn_w_out, 0)])
    h = _proj_res_ln(o.reshape(bsz * n, d), w_o, h, g1, ln_g[0, 0], ln_b[0, 0], n)
    h = _ffn(h, sh2, sc2, g2, ffn_w_in0, ffn_w_out0, ln_g[0, 1], ln_b[0, 1], n)

    (sh1, sc1, g1, sh2, sc2, g2), _ = mod_vectors(1)
    b_s_full = jnp.repeat(jnp.transpose(gm_b_s[0]), GROUP_DIM, axis=1)
    uv, ffn_w_in1, ffn_w_out1 = _gmlp_gate(h, sh1, sc1, gm_w_in_bf, gm_ln_g[0], gm_ln_b[0], gm_w_s[0].astype(BF16),
                                            b_s_full, n, cast_srcs=[(ffn_w_in, 1), (ffn_w_out, 1)])
    h = _proj_res_ln(uv, gm_w_out_bf, h, g1, ln_g[1, 0], ln_b[1, 0], n)
    h = _ffn(h, sh2, sc2, g2, ffn_w_in1, ffn_w_out1, ln_g[1, 1], ln_b[1, 1], n)
    return h.reshape(bsz, n, d)
```

```python
import functools
from typing import NamedTuple

import numpy as np
import jax
import jax.numpy as jnp
from jax import lax
from jax.experimental import pallas as pl
from jax.experimental.pallas import tpu as pltpu

D_MODEL = 2048
GRID_W = 64
N_HEADS = 16
HEAD_DIM = D_MODEL // N_HEADS
WIN_H = 8
WIN_W = 16
CHUNK = 128
N_GROUPS = 16
GROUP_DIM = D_MODEL // N_GROUPS
DEPTH = 2
ALPHA = (2 * DEPTH) ** 0.25
LN_EPS = 1e-5

VMEM_LIMIT_BYTES = 60 * 1024 * 1024
MASK_VALUE = -1e30
LOG2_E = 1.4426950408889634

Q_ROWS = 2
KEY_ROWS = Q_ROWS + WIN_H - 1
Q_BLK = Q_ROWS * GRID_W
K_BLK = KEY_ROWS * GRID_W
GMLP_COL_CHUNK = 512
GMLP_SUB_ROWS = 256
PROJ_SUB_BLOCKS = 4
ATTN_GROUP = 4
ATTN_ITEMS = 2
STRIP_PAD_TILES = 1

BF16 = jnp.bfloat16
F32 = jnp.float32
BF16_SUBLANES = 16


def _params(*sem):
    return pltpu.CompilerParams(dimension_semantics=sem, vmem_limit_bytes=VMEM_LIMIT_BYTES)


def _layer_norm(x, g, b):
    mu = jnp.mean(x, axis=-1, keepdims=True)
    xc = x - mu
    var = jnp.mean(xc * xc, axis=-1, keepdims=True)
    return xc * lax.rsqrt(var + LN_EPS) * g + b


def _silu(x):
    return x * (1.0 / (1.0 + jnp.exp(-x)))


def _dot(a, b):
    return jnp.dot(a, b, preferred_element_type=F32)


class _CastJob(NamedTuple):
    src: jax.Array
    layer: int
    rows: int
    steps_per_block: int


def _cast_job(src, layer, n_steps):
    r = src.shape[1]
    for steps_per_block in (1, 2, 4, 8):
        rows, rem = divmod(r * steps_per_block, n_steps)
        if rem == 0 and rows % BF16_SUBLANES == 0:
            return _CastJob(src, layer, rows, steps_per_block)
    raise ValueError(f"cannot split {r} rows over {n_steps} steps")


def _cast_io(jobs, step_of):
    in_specs, out_specs, out_shapes = [], [], []
    for job in jobs:
        _, r, c = job.src.shape
        in_specs.append(pl.BlockSpec(
            (None, job.rows, c), lambda *g, job=job: (job.layer, step_of(*g) // job.steps_per_block, 0)))
        out_specs.append(pl.BlockSpec(
            (job.rows, c), lambda *g, job=job: (step_of(*g) // job.steps_per_block, 0)))
        out_shapes.append(jax.ShapeDtypeStruct((r, c), BF16))
    return in_specs, out_specs, out_shapes


def _run_casts(src_refs, dst_refs):
    for src, dst in zip(src_refs, dst_refs):
        dst[...] = src[...].astype(dst.dtype)


def _ada_kernel(*refs, n_cast):
    cond_ref, w_ref, b_ref = refs[:3]
    o_ref = refs[3 + n_cast]
    s = _silu(cond_ref[...]).astype(BF16)
    o_ref[0] = _dot(s, w_ref[0].astype(BF16)) + b_ref[0]
    _run_casts(refs[3:3 + n_cast], refs[4 + n_cast:4 + 2 * n_cast])


def _ada_params(cond, ada_w, ada_b, tn=768, cast_srcs=()):
    depth, d, n6 = ada_w.shape
    rows = cond.shape[0]
    grid = (depth, n6 // tn)
    jobs = [_cast_job(src, layer, grid[0] * grid[1]) for src, layer in cast_srcs]
    cast_in, cast_out, cast_shapes = _cast_io(jobs, lambda l, j: l * grid[1] + j)
    return pl.pallas_call(
        functools.partial(_ada_kernel, n_cast=len(jobs)),
        grid=grid,
        in_specs=[
            pl.BlockSpec((rows, d), lambda l, j: (0, 0)),
            pl.BlockSpec((1, d, tn), lambda l, j: (l, 0, j)),
            pl.BlockSpec((1, 1, tn), lambda l, j: (l, 0, j)),
        ] + cast_in,
        out_specs=[pl.BlockSpec((1, rows, tn), lambda l, j: (l, 0, j))] + cast_out,
        out_shape=[jax.ShapeDtypeStruct((depth, rows, n6), F32)] + cast_shapes,
        compiler_params=_params("arbitrary", "arbitrary"),
        name="ada_params",
    )(cond, ada_w, ada_b.reshape(depth, 1, n6), *[job.src for job in jobs])


def _mod_matmul_kernel(*refs, n_cast):
    x_ref, sh_ref, sc_ref, w_ref, cs_ref = refs[:5]
    cast_src = refs[5:5 + n_cast]
    o_ref = refs[5 + n_cast]
    cast_dst = refs[6 + n_cast:6 + 2 * n_cast]
    a_ref = refs[6 + 2 * n_cast]

    @pl.when(pl.program_id(1) == 0)
    def _():
        a_ref[...] = (x_ref[...] * (1.0 + sc_ref[0]) + sh_ref[0]).astype(BF16)

    o_ref[...] = (_dot(a_ref[...], w_ref[...]) * cs_ref[...]).astype(o_ref.dtype)
    _run_casts(cast_src, cast_dst)


def _mod_matmul(x, sh, sc, w, col_scale, rows_per_batch, tm, tn, col_start=0, cast_srcs=(), name="mod_matmul"):
    m, d = x.shape
    n = w.shape[1] - col_start
    col_blk = col_start // tn
    tiles_per_batch = rows_per_batch // tm
    grid = (m // tm, n // tn)
    jobs = [_cast_job(src, layer, grid[0] * grid[1]) for src, layer in cast_srcs]
    cast_in, cast_out, cast_shapes = _cast_io(jobs, lambda i, j: i * grid[1] + j)
    if sh.shape[0] == 1:
        mod_idx = lambda i, j: (0, 0, 0)
    else:
        mod_idx = lambda i, j: (i // tiles_per_batch, 0, 0)
    return pl.pallas_call(
        functools.partial(_mod_matmul_kernel, n_cast=len(jobs)),
        grid=grid,
        in_specs=[
            pl.BlockSpec((tm, d), lambda i, j: (i, 0)),
            pl.BlockSpec((1, 1, d), mod_idx),
            pl.BlockSpec((1, 1, d), mod_idx),
            pl.BlockSpec((d, tn), lambda i, j: (0, col_blk + j)),
            pl.BlockSpec((1, tn), lambda i, j: (0, j)),
        ] + cast_in,
        out_specs=[pl.BlockSpec((tm, tn), lambda i, j: (i, j))] + cast_out,
        out_shape=[jax.ShapeDtypeStruct((m, n), BF16)] + cast_shapes,
        scratch_shapes=[pltpu.VMEM((tm, d), BF16)],
        compiler_params=_params("parallel", "arbitrary"),
        name=name,
    )(x, sh, sc, w, col_scale, *[job.src for job in jobs])


def _attn_geometries(rows):
    n_blocks = rows // Q_ROWS
    last_kb = rows - KEY_ROWS
    geoms = [(0, 0), (Q_ROWS, 0), (2 * Q_ROWS, 0),
             ((n_blocks - 2) * Q_ROWS, last_kb), ((n_blocks - 1) * Q_ROWS, last_kb)]
    layout = []
    for rb, kb in geoms:
        per_row = []
        for dr in range(Q_ROWS):
            r = rb + dr
            r0 = min(max(r - WIN_H // 2, 0), rows - WIN_H)
            first_tile = kb - r + (WIN_H - 1) + STRIP_PAD_TILES
            assert 0 <= first_tile and first_tile + KEY_ROWS <= 2 * WIN_H - 1 + 2 * STRIP_PAD_TILES
            per_row.append((first_tile, max(r0 - kb, 0), min(r0 - kb + WIN_H, KEY_ROWS)))
        layout.append(per_row)
    return layout


def _attn_bias_strip(rpb):
    n_heads, _, n_col_rel = rpb.shape
    c = np.arange(GRID_W)[:, None]
    kc = np.arange(GRID_W)[None, :]
    c0 = np.clip(c - WIN_W // 2, 0, GRID_W - WIN_W)
    col_ok = (kc >= c0) & (kc < c0 + WIN_W)
    select = (kc - c + WIN_W - 1)[None] == np.arange(n_col_rel)[:, None, None]
    select = jnp.asarray((select & col_ok[None]).astype(np.float32))
    tiles = jnp.einsum("hab,bck->hcak", rpb * LOG2_E, select, precision=lax.Precision.HIGHEST)
    tiles = jnp.where(col_ok[None, :, None, :], tiles, MASK_VALUE)
    masked = jnp.full((n_heads, GRID_W, STRIP_PAD_TILES, GRID_W), MASK_VALUE, rpb.dtype)
    return jnp.concatenate([masked, tiles, masked], axis=2).reshape(n_heads, GRID_W, -1)


def _attn_kernel(*refs, n_blocks, last_kb, n_cast, layout):
    q_ref, k_ref, v_ref, kc_ref, vc_ref, strip_ref = refs[:6]
    o_ref = refs[6 + n_cast]
    s0, s1, sc0, sc1, p0, p1, pc0, pc1, den0, den1, bias_ref = refs[7 + 2 * n_cast:]
    _run_casts(refs[6:6 + n_cast], refs[7 + n_cast:7 + 2 * n_cast])

    @pl.when(pl.program_id(1) == 0)
    def _():
        key_row = lax.broadcasted_iota(jnp.int32, (GRID_W, K_BLK), 1) // GRID_W
        for geom, per_row in enumerate(layout):
            for dr, (first_tile, lo, hi) in enumerate(per_row):
                window = strip_ref[0, :, first_tile * GRID_W:first_tile * GRID_W + K_BLK]
                valid = (key_row >= lo) & (key_row < hi)
                bias_ref[geom, dr * GRID_W:(dr + 1) * GRID_W, :] = jnp.where(valid, window, MASK_VALUE)

    _attn_pipeline(q_ref, k_ref, v_ref, kc_ref, vc_ref, bias_ref, o_ref,
                   s0, s1, sc0, sc1, p0, p1, pc0, pc1, den0, den1, n_blocks=n_blocks, last_kb=last_kb)


def _attn_pipeline(q_ref, k_ref, v_ref, kc_ref, vc_ref, bias_ref, o_ref,
                   s0, s1, sc0, sc1, p0, p1, pc0, pc1, den0, den1, *, n_blocks, last_kb):
    s_bufs, sc_bufs, p_bufs, pc_bufs, den_bufs = (s0, s1), (sc0, sc1), (p0, p1), (pc0, pc1), (den0, den1)
    groups_per_item = n_blocks // ATTN_GROUP
    n_groups = q_ref.shape[0] * groups_per_item
    g_rows = ATTN_GROUP * Q_BLK
    contract_last = (((1,), (1,)), ((), ()))

    def key_start(t):
        kb = jnp.clip(Q_ROWS * t - WIN_H // 2, 0, last_kb)
        return pl.multiple_of(kb * GRID_W, GRID_W)

    def locate(g):
        item = g // groups_per_item
        local = g - item * groups_per_item
        return item, local, pl.multiple_of(local * g_rows, g_rows)

    def scores(g, slot):
        item, local, row0 = locate(g)
        q = q_ref[item, pl.ds(row0, g_rows), :]
        sc_bufs[slot][...] = lax.dot_general(q, kc_ref[item], contract_last, preferred_element_type=F32)
        for i in range(ATTN_GROUP):
            t = local * ATTN_GROUP + i
            geom = jnp.where(t < 2, t, jnp.where(t >= n_blocks - 2, t - (n_blocks - 5), 2))
            k = k_ref[item, pl.ds(key_start(t), K_BLK), :]
            s = lax.dot_general(q[i * Q_BLK:(i + 1) * Q_BLK], k, contract_last, preferred_element_type=F32)
            s_bufs[slot][i * Q_BLK:(i + 1) * Q_BLK, :] = s + bias_ref[geom]

    def softmax(slot):
        s = s_bufs[slot][...]
        sc = sc_bufs[slot][...]
        m = jnp.maximum(jnp.max(s, axis=-1, keepdims=True), jnp.max(sc, axis=-1, keepdims=True))
        p = jnp.exp2(s - m)
        pc = jnp.exp2(sc - m)
        den_bufs[slot][...] = jnp.sum(p, axis=-1, keepdims=True) + jnp.sum(pc, axis=-1, keepdims=True)
        p_bufs[slot][...] = p.astype(BF16)
        pc_bufs[slot][...] = pc.astype(BF16)

    def weighted_sum(g, slot):
        item, local, row0 = locate(g)
        o_ctx = _dot(pc_bufs[slot][...], vc_ref[item])
        outs = []
        for i in range(ATTN_GROUP):
            v = v_ref[item, pl.ds(key_start(local * ATTN_GROUP + i), K_BLK), :]
            outs.append(_dot(p_bufs[slot][i * Q_BLK:(i + 1) * Q_BLK, :], v))
        o = (jnp.concatenate(outs, axis=0) + o_ctx) / den_bufs[slot][...]
        o_ref[item, pl.ds(row0, g_rows), :] = o.astype(o_ref.dtype)

    scores(0, 0)
    scores(1, 1)
    softmax(0)

    def steady(j, carry):
        for parity in range(2):
            tau = 2 * j + 2 + parity
            scores(tau, parity)
            softmax(1 - parity)
            weighted_sum(tau - 2, parity)
        return carry

    lax.fori_loop(0, (n_groups - 2) // 2, steady, 0)
    softmax(1)
    weighted_sum(n_groups - 2, 0)
    weighted_sum(n_groups - 1, 1)


def _neighbourhood_attention(qkv, kvc, strip, cast_srcs=()):
    bsz, n, _ = qkv.shape
    ctx_len = kvc.shape[1]
    rows = n // GRID_W
    n_blocks = rows // Q_ROWS
    assert n_blocks % (2 * ATTN_GROUP) == 0 and n_blocks // ATTN_GROUP >= 4
    assert bsz % ATTN_ITEMS == 0
    n_batch_steps = bsz // ATTN_ITEMS
    jobs = [_cast_job(src, layer, N_HEADS * n_batch_steps) for src, layer in cast_srcs]
    cast_in, cast_out, cast_shapes = _cast_io(jobs, lambda h, b: h * n_batch_steps + b)
    layout = _attn_geometries(rows)
    kern = functools.partial(_attn_kernel, n_blocks=n_blocks, last_kb=rows - KEY_ROWS, n_cast=len(jobs), layout=layout)
    seq_spec = lambda off: pl.BlockSpec((ATTN_ITEMS, n, HEAD_DIM), lambda h, b: (b, 0, off + h))
    ctx_spec = lambda off: pl.BlockSpec((ATTN_ITEMS, ctx_len, HEAD_DIM), lambda h, b: (b, 0, off + h))
    g_rows = ATTN_GROUP * Q_BLK
    slot_pair = lambda cols, dtype: [pltpu.VMEM((g_rows, cols), dtype)] * 2
    return pl.pallas_call(
        kern,
        grid=(N_HEADS, n_batch_steps),
        in_specs=[
            seq_spec(0), seq_spec(N_HEADS), seq_spec(2 * N_HEADS),
            ctx_spec(0), ctx_spec(N_HEADS),
            pl.BlockSpec((1,) + strip.shape[1:], lambda h, b: (h, 0, 0)),
        ] + cast_in,
        out_specs=[pl.BlockSpec((ATTN_ITEMS, n, HEAD_DIM), lambda h, b: (b, 0, h))] + cast_out,
        out_shape=[jax.ShapeDtypeStruct((bsz, n, D_MODEL), BF16)] + cast_shapes,
        scratch_shapes=(slot_pair(K_BLK, F32) + slot_pair(ctx_len, F32) + slot_pair(K_BLK, BF16)
                        + slot_pair(ctx_len, BF16) + slot_pair(1, F32)
                        + [pltpu.VMEM((len(layout), Q_BLK, K_BLK), F32)]),
        compiler_params=_params("arbitrary", "arbitrary"),
        name="neighbourhood_attention",
    )(qkv, qkv, qkv, kvc, kvc, strip, *[job.src for job in jobs])


def _proj_res_ln_kernel(y_ref, w_ref, h_ref, g_ref, lg_ref, lb_ref, o_ref):
    sub = y_ref.shape[0] // PROJ_SUB_BLOCKS
    for r in range(PROJ_SUB_BLOCKS):
        rows = slice(r * sub, (r + 1) * sub)
        y = _dot(y_ref[rows, :], w_ref[...])
        o_ref[rows, :] = _layer_norm(ALPHA * h_ref[rows, :] + g_ref[0] * y, lg_ref[...], lb_ref[...])


def _proj_res_ln(y, w, h, gate, ln_g, ln_b, rows_per_batch, tm=1024):
    m, k = y.shape
    d = w.shape[1]
    tiles_per_batch = rows_per_batch // tm
    return pl.pallas_call(
        _proj_res_ln_kernel,
        grid=(m // tm,),
        in_specs=[
            pl.BlockSpec((tm, k), lambda i: (i, 0)),
            pl.BlockSpec((k, d), lambda i: (0, 0), pipeline_mode=pl.Buffered(1)),
            pl.BlockSpec((tm, d), lambda i: (i, 0)),
            pl.BlockSpec((1, 1, d), lambda i: (i // tiles_per_batch, 0, 0)),
            pl.BlockSpec((1, d), lambda i: (0, 0)),
            pl.BlockSpec((1, d), lambda i: (0, 0)),
        ],
        out_specs=pl.BlockSpec((tm, d), lambda i: (i, 0)),
        out_shape=jax.ShapeDtypeStruct((m, d), F32),
        compiler_params=_params("parallel"),
        name="proj_res_ln",
    )(y, w, h, gate, ln_g.reshape(1, d), ln_b.reshape(1, d))


def _ffn_kernel(h_ref, sh_ref, sc_ref, g_ref, wg_ref, wu_ref, wo_ref, lg_ref, lb_ref, o_ref, a_ref):
    f = pl.program_id(1)

    @pl.when(f == 0)
    def _():
        h = h_ref[...]
        a_ref[...] = (h * (1.0 + sc_ref[0]) + sh_ref[0]).astype(BF16)
        o_ref[...] = ALPHA * h

    a = a_ref[...]
    act = _silu(_dot(a, wg_ref[...])) * _dot(a, wu_ref[...])
    o_ref[...] += g_ref[0] * _dot(act.astype(BF16), wo_ref[...])

    @pl.when(f == pl.num_programs(1) - 1)
    def _():
        o_ref[...] = _layer_norm(o_ref[...], lg_ref[...], lb_ref[...])


def _ffn(h, sh, sc, gate, w_in, w_out, ln_g, ln_b, rows_per_batch, tm=1024, tf=512):
    m, d = h.shape
    d_ff = w_out.shape[0]
    n_f = d_ff // tf
    tiles_per_batch = rows_per_batch // tm
    mod_spec = pl.BlockSpec((1, 1, d), lambda i, f: (i // tiles_per_batch, 0, 0))
    vec_spec = pl.BlockSpec((1, d), lambda i, f: (0, 0))
    return pl.pallas_call(
        _ffn_kernel,
        grid=(m // tm, n_f),
        in_specs=[
            pl.BlockSpec((tm, d), lambda i, f: (i, 0)),
            mod_spec, mod_spec, mod_spec,
            pl.BlockSpec((d, tf), lambda i, f: (0, f)),
            pl.BlockSpec((d, tf), lambda i, f: (0, n_f + f)),
            pl.BlockSpec((tf, d), lambda i, f: (f, 0)),
            vec_spec, vec_spec,
        ],
        out_specs=pl.BlockSpec((tm, d), lambda i, f: (i, 0)),
        out_shape=jax.ShapeDtypeStruct((m, d), F32),
        scratch_shapes=[pltpu.VMEM((tm, d), BF16)],
        compiler_params=_params("parallel", "arbitrary"),
        name="swiglu_ffn",
    )(h, sh, sc, gate, w_in, w_in, w_out, ln_g.reshape(1, d), ln_b.reshape(1, d))


def _gmlp_kernel(*refs, n_cast):
    h_ref, sh_ref, sc_ref, wu_ref, wv_ref, lg_ref, lb_ref, ws_ref, bs_ref = refs[:9]
    o_ref = refs[9 + n_cast]
    u_ref, v_ref = refs[10 + 2 * n_cast:]
    _run_casts(refs[9:9 + n_cast], refs[10 + n_cast:10 + 2 * n_cast])

    def gelu(z):
        return 0.5 * z * (1.0 + lax.erf(z * (2.0 ** -0.5)))

    d = u_ref.shape[1]
    col_chunks = [slice(c * GMLP_COL_CHUNK, (c + 1) * GMLP_COL_CHUNK) for c in range(d // GMLP_COL_CHUNK)]
    for r0 in range(0, h_ref.shape[0], GMLP_SUB_ROWS):
        sub = slice(r0, r0 + GMLP_SUB_ROWS)
        a = (h_ref[sub, :] * (1.0 + sc_ref[0]) + sh_ref[0]).astype(BF16)
        v = jnp.concatenate([gelu(_dot(a, wv_ref[:, cols])) for cols in col_chunks], axis=-1)
        v_ref[sub, :] = _layer_norm(v, lg_ref[...], lb_ref[...]).astype(BF16)
        for cols in col_chunks:
            u_ref[sub, cols] = gelu(_dot(a, wu_ref[:, cols]))
        for c0 in range(r0, r0 + GMLP_SUB_ROWS, CHUNK):
            rows = slice(c0, c0 + CHUNK)
            for g in range(N_GROUPS):
                cols = slice(g * GROUP_DIM, (g + 1) * GROUP_DIM)
                mixed = _dot(ws_ref[g], v_ref[rows, cols]) + bs_ref[:, cols]
                o_ref[rows, cols] = (u_ref[rows, cols] * mixed).astype(o_ref.dtype)


def _gmlp_gate(h, sh, sc, w_in, ln_g, ln_b, w_s, b_s_full, rows_per_batch, tm=512, cast_srcs=()):
    m, d = h.shape
    tiles_per_batch = rows_per_batch // tm
    mod_spec = pl.BlockSpec((1, 1, d), lambda i: (i // tiles_per_batch, 0, 0))
    vec_spec = pl.BlockSpec((1, d), lambda i: (0, 0))
    jobs = [_cast_job(src, layer, m // tm) for src, layer in cast_srcs]
    cast_in, cast_out, cast_shapes = _cast_io(jobs, lambda i: i)
    return pl.pallas_call(
        functools.partial(_gmlp_kernel, n_cast=len(jobs)),
        grid=(m // tm,),
        in_specs=[
            pl.BlockSpec((tm, d), lambda i: (i, 0)),
            mod_spec, mod_spec,
            pl.BlockSpec((d, d), lambda i: (0, 0)),
            pl.BlockSpec((d, d), lambda i: (0, 1)),
            vec_spec, vec_spec,
            pl.BlockSpec((N_GROUPS, CHUNK, CHUNK), lambda i: (0, 0, 0)),
            pl.BlockSpec((CHUNK, d), lambda i: (0, 0)),
        ] + cast_in,
        out_specs=[pl.BlockSpec((tm, d), lambda i: (i, 0))] + cast_out,
        out_shape=[jax.ShapeDtypeStruct((m, d), BF16)] + cast_shapes,
        scratch_shapes=[pltpu.VMEM((tm, d), F32), pltpu.VMEM((tm, d), BF16)],
        compiler_params=_params("arbitrary"),
        name="gmlp_gate",
    )(h, sh, sc, w_in, w_in, ln_g.reshape(1, d), ln_b.reshape(1, d), w_s, b_s_full, *[job.src for job in jobs])


def kernel(x, c, ctx, c_ctx, ada_w, ada_b, ln_g, ln_b, na_w_qkv, na_w_o, na_rpb, gm_w_in, gm_ln_g, gm_ln_b,
           gm_w_s, gm_b_s, gm_w_out, ffn_w_in, ffn_w_out):
    bsz, n, d = x.shape
    ctx_len = ctx.shape[1]
    rows = n // GRID_W
    assert d == D_MODEL and n % (GRID_W * Q_ROWS) == 0 and rows >= KEY_ROWS + 2 * Q_ROWS

    cond = jnp.concatenate([c, c_ctx[None, :], jnp.zeros((8 - bsz - 1, d), F32)], axis=0)
    ada, w_qkv = _ada_params(cond, ada_w, ada_b, cast_srcs=[(na_w_qkv, 0)])

    def mod_vectors(layer):
        parts = jnp.split(ada[layer], 6, axis=-1)
        latent = [p[:bsz].reshape(bsz, 1, d) for p in parts]
        context = [p[bsz:bsz + 1].reshape(1, 1, d) for p in parts]
        return latent, context

    h = x.reshape(bsz * n, d)

    (sh1, sc1, g1, sh2, sc2, g2), (csh1, csc1, _, _, _, _) = mod_vectors(0)
    q_scale = jnp.concatenate([jnp.full((1, d), HEAD_DIM ** -0.5 * LOG2_E, F32), jnp.ones((1, 2 * d), F32)], axis=1)
    qkv, = _mod_matmul(h, sh1, sc1, w_qkv, q_scale, n, tm=1024, tn=2048, name="qkv_proj")
    kvc, = _mod_matmul(ctx.reshape(bsz * ctx_len, d), csh1, csc1, w_qkv, jnp.ones((1, 2 * d), F32), ctx_len,
                       tm=bsz * ctx_len, tn=1024, col_start=d, name="ctx_kv_proj")
    strip = _attn_bias_strip(na_rpb[0])
    o, w_o, gm_w_in_bf, gm_w_out_bf, ffn_w_in0, ffn_w_out0 = _neighbourhood_attention(
        qkv.reshape(bsz, n, 3 * d), kvc.reshape(bsz, ctx_len, 2 * d), strip,
        cast_srcs=[(na_w_o, 0), (gm_w_in, 0), (gm_w_out, 0), (ffn_w_in, 0), (ffn_w_out, 0)])
    h = _proj_res_ln(o.reshape(bsz * n, d), w_o, h, g1, ln_g[0, 0], ln_b[0, 0], n)
    h = _ffn(h, sh2, sc2, g2, ffn_w_in0, ffn_w_out0, ln_g[0, 1], ln_b[0, 1], n)

    (sh1, sc1, g1, sh2, sc2, g2), _ = mod_vectors(1)
    b_s_full = jnp.repeat(jnp.transpose(gm_b_s[0]), GROUP_DIM, axis=1)
    uv, ffn_w_in1, ffn_w_out1 = _gmlp_gate(h, sh1, sc1, gm_w_in_bf, gm_ln_g[0], gm_ln_b[0], gm_w_s[0].astype(BF16),
                                            b_s_full, n, cast_srcs=[(ffn_w_in, 1), (ffn_w_out, 1)])
    h = _proj_res_ln(uv, gm_w_out_bf, h, g1, ln_g[1, 0], ln_b[1, 0], n)
    h = _ffn(h, sh2, sc2, g2, ffn_w_in1, ffn_w_out1, ln_g[1, 1], ln_b[1, 1], n)
    return h.reshape(bsz, n, d)
```

```python
import functools
from typing import NamedTuple

import numpy as np
import jax
import jax.numpy as jnp
from jax import lax
from jax.experimental import pallas as pl
from jax.experimental.pallas import tpu as pltpu

D_MODEL = 2048
GRID_W = 64
N_HEADS = 16
HEAD_DIM = D_MODEL // N_HEADS
WIN_H = 8
WIN_W = 16
CHUNK = 128
N_GROUPS = 16
GROUP_DIM = D_MODEL // N_GROUPS
DEPTH = 2
ALPHA = (2 * DEPTH) ** 0.25
LN_EPS = 1e-5

VMEM_LIMIT_BYTES = 60 * 1024 * 1024
MASK_VALUE = -1e30
LOG2_E = 1.4426950408889634

Q_ROWS = 2
KEY_ROWS = Q_ROWS + WIN_H - 1
Q_BLK = Q_ROWS * GRID_W
K_BLK = KEY_ROWS * GRID_W
GMLP_COL_CHUNK = 512
GMLP_SUB_ROWS = 256
PROJ_SUB_BLOCKS = 4
ATTN_GROUP = 8
ATTN_ITEMS = 2
STRIP_PAD_TILES = 1

BF16 = jnp.bfloat16
F32 = jnp.float32
BF16_SUBLANES = 16
LN_STRIP_ROWS = 8


def _params(*sem):
    return pltpu.CompilerParams(dimension_semantics=sem, vmem_limit_bytes=VMEM_LIMIT_BYTES)


def _layer_norm(x, g, b):
    mu = jnp.mean(x, axis=-1, keepdims=True)
    xc = x - mu
    var = jnp.mean(xc * xc, axis=-1, keepdims=True)
    return xc * lax.rsqrt(var + LN_EPS) * g + b


def _layer_norm_in_place(ref, rows, g, b):
    for r0 in range(rows.start, rows.stop, LN_STRIP_ROWS):
        strip = slice(r0, r0 + LN_STRIP_ROWS)
        ref[strip, :] = _layer_norm(ref[strip, :], g, b)


def _silu(x):
    return x * (1.0 / (1.0 + jnp.exp(-x)))


def _dot(a, b):
    return jnp.dot(a, b, preferred_element_type=F32)


class _CastJob(NamedTuple):
    src: jax.Array
    layer: int
    rows: int
    steps_per_block: int


def _cast_job(src, layer, n_steps):
    r = src.shape[1]
    for steps_per_block in (1, 2, 4, 8):
        rows, rem = divmod(r * steps_per_block, n_steps)
        if rem == 0 and rows % BF16_SUBLANES == 0:
            return _CastJob(src, layer, rows, steps_per_block)
    raise ValueError(f"cannot split {r} rows over {n_steps} steps")


def _cast_io(jobs, step_of):
    in_specs, out_specs, out_shapes = [], [], []
    for job in jobs:
        _, r, c = job.src.shape
        in_specs.append(pl.BlockSpec(
            (None, job.rows, c), lambda *g, job=job: (job.layer, step_of(*g) // job.steps_per_block, 0)))
        out_specs.append(pl.BlockSpec(
            (job.rows, c), lambda *g, job=job: (step_of(*g) // job.steps_per_block, 0)))
        out_shapes.append(jax.ShapeDtypeStruct((r, c), BF16))
    return in_specs, out_specs, out_shapes


def _run_casts(src_refs, dst_refs):
    for src, dst in zip(src_refs, dst_refs):
        dst[...] = src[...].astype(dst.dtype)


def _ada_kernel(*refs, n_cast):
    cond_ref, w_ref, b_ref = refs[:3]
    o_ref = refs[3 + n_cast]
    s = _silu(cond_ref[...]).astype(BF16)
    o_ref[0] = _dot(s, w_ref[0].astype(BF16)) + b_ref[0]
    _run_casts(refs[3:3 + n_cast], refs[4 + n_cast:4 + 2 * n_cast])


def _ada_params(cond, ada_w, ada_b, tn=768, cast_srcs=()):
    depth, d, n6 = ada_w.shape
    rows = cond.shape[0]
    grid = (depth, n6 // tn)
    jobs = [_cast_job(src, layer, grid[0] * grid[1]) for src, layer in cast_srcs]
    cast_in, cast_out, cast_shapes = _cast_io(jobs, lambda l, j: l * grid[1] + j)
    return pl.pallas_call(
        functools.partial(_ada_kernel, n_cast=len(jobs)),
        grid=grid,
        in_specs=[
            pl.BlockSpec((rows, d), lambda l, j: (0, 0)),
            pl.BlockSpec((1, d, tn), lambda l, j: (l, 0, j)),
            pl.BlockSpec((1, 1, tn), lambda l, j: (l, 0, j)),
        ] + cast_in,
        out_specs=[pl.BlockSpec((1, rows, tn), lambda l, j: (l, 0, j))] + cast_out,
        out_shape=[jax.ShapeDtypeStruct((depth, rows, n6), F32)] + cast_shapes,
        compiler_params=_params("arbitrary", "arbitrary"),
        name="ada_params",
    )(cond, ada_w, ada_b.reshape(depth, 1, n6), *[job.src for job in jobs])


def _mod_matmul_kernel(*refs, n_cast):
    x_ref, sh_ref, sc_ref, w_ref, cs_ref = refs[:5]
    cast_src = refs[5:5 + n_cast]
    o_ref = refs[5 + n_cast]
    cast_dst = refs[6 + n_cast:6 + 2 * n_cast]
    a_ref = refs[6 + 2 * n_cast]

    @pl.when(pl.program_id(1) == 0)
    def _():
        a_ref[...] = (x_ref[...] * (1.0 + sc_ref[0]) + sh_ref[0]).astype(BF16)

    o_ref[...] = (_dot(a_ref[...], w_ref[...]) * cs_ref[...]).astype(o_ref.dtype)
    _run_casts(cast_src, cast_dst)


def _mod_matmul(x, sh, sc, w, col_scale, rows_per_batch, tm, tn, col_start=0, cast_srcs=(), name="mod_matmul"):
    m, d = x.shape
    n = w.shape[1] - col_start
    col_blk = col_start // tn
    tiles_per_batch = rows_per_batch // tm
    grid = (m // tm, n // tn)
    jobs = [_cast_job(src, layer, grid[0] * grid[1]) for src, layer in cast_srcs]
    cast_in, cast_out, cast_shapes = _cast_io(jobs, lambda i, j: i * grid[1] + j)
    if sh.shape[0] == 1:
        mod_idx = lambda i, j: (0, 0, 0)
    else:
        mod_idx = lambda i, j: (i // tiles_per_batch, 0, 0)
    return pl.pallas_call(
        functools.partial(_mod_matmul_kernel, n_cast=len(jobs)),
        grid=grid,
        in_specs=[
            pl.BlockSpec((tm, d), lambda i, j: (i, 0)),
            pl.BlockSpec((1, 1, d), mod_idx),
            pl.BlockSpec((1, 1, d), mod_idx),
            pl.BlockSpec((d, tn), lambda i, j: (0, col_blk + j)),
            pl.BlockSpec((1, tn), lambda i, j: (0, j)),
        ] + cast_in,
        out_specs=[pl.BlockSpec((tm, tn), lambda i, j: (i, j))] + cast_out,
        out_shape=[jax.ShapeDtypeStruct((m, n), BF16)] + cast_shapes,
        scratch_shapes=[pltpu.VMEM((tm, d), BF16)],
        compiler_params=_params("parallel", "arbitrary"),
        name=name,
    )(x, sh, sc, w, col_scale, *[job.src for job in jobs])


def _attn_geometries(rows):
    n_blocks = rows // Q_ROWS
    last_kb = rows - KEY_ROWS
    geoms = [(0, 0), (Q_ROWS, 0), (2 * Q_ROWS, 0),
             ((n_blocks - 2) * Q_ROWS, last_kb), ((n_blocks - 1) * Q_ROWS, last_kb)]
    layout = []
    for rb, kb in geoms:
        per_row = []
        for dr in range(Q_ROWS):
            r = rb + dr
            r0 = min(max(r - WIN_H // 2, 0), rows - WIN_H)
            first_tile = kb - r + (WIN_H - 1) + STRIP_PAD_TILES
            assert 0 <= first_tile and first_tile + KEY_ROWS <= 2 * WIN_H - 1 + 2 * STRIP_PAD_TILES
            per_row.append((first_tile, max(r0 - kb, 0), min(r0 - kb + WIN_H, KEY_ROWS)))
        layout.append(per_row)
    return layout


def _attn_bias_strip(rpb):
    n_heads, _, n_col_rel = rpb.shape
    c = np.arange(GRID_W)[:, None]
    kc = np.arange(GRID_W)[None, :]
    c0 = np.clip(c - WIN_W // 2, 0, GRID_W - WIN_W)
    col_ok = (kc >= c0) & (kc < c0 + WIN_W)
    select = (kc - c + WIN_W - 1)[None] == np.arange(n_col_rel)[:, None, None]
    select = jnp.asarray((select & col_ok[None]).astype(np.float32))
    tiles = jnp.einsum("hab,bck->hcak", rpb * LOG2_E, select, precision=lax.Precision.HIGHEST)
    tiles = jnp.where(col_ok[None, :, None, :], tiles, MASK_VALUE)
    masked = jnp.full((n_heads, GRID_W, STRIP_PAD_TILES, GRID_W), MASK_VALUE, rpb.dtype)
    return jnp.concatenate([masked, tiles, masked], axis=2).reshape(n_heads, GRID_W, -1)


def _attn_kernel(*refs, n_blocks, last_kb, n_cast, layout):
    q_ref, k_ref, v_ref, kc_ref, vc_ref, strip_ref = refs[:6]
    o_ref = refs[6 + n_cast]
    s0, s1, sc0, sc1, p0, p1, pc0, pc1, den0, den1, bias_ref = refs[7 + 2 * n_cast:]
    _run_casts(refs[6:6 + n_cast], refs[7 + n_cast:7 + 2 * n_cast])

    @pl.when(pl.program_id(1) == 0)
    def _():
        key_row = lax.broadcasted_iota(jnp.int32, (GRID_W, K_BLK), 1) // GRID_W
        for geom, per_row in enumerate(layout):
            for dr, (first_tile, lo, hi) in enumerate(per_row):
                window = strip_ref[0, :, first_tile * GRID_W:first_tile * GRID_W + K_BLK]
                valid = (key_row >= lo) & (key_row < hi)
                bias_ref[geom, dr * GRID_W:(dr + 1) * GRID_W, :] = jnp.where(valid, window, MASK_VALUE)

    _attn_pipeline(q_ref, k_ref, v_ref, kc_ref, vc_ref, bias_ref, o_ref,
                   s0, s1, sc0, sc1, p0, p1, pc0, pc1, den0, den1, n_blocks=n_blocks, last_kb=last_kb)


def _attn_pipeline(q_ref, k_ref, v_ref, kc_ref, vc_ref, bias_ref, o_ref,
                   s0, s1, sc0, sc1, p0, p1, pc0, pc1, den0, den1, *, n_blocks, last_kb):
    s_bufs, sc_bufs, p_bufs, pc_bufs, den_bufs = (s0, s1), (sc0, sc1), (p0, p1), (pc0, pc1), (den0, den1)
    groups_per_item = n_blocks // ATTN_GROUP
    n_groups = q_ref.shape[0] * groups_per_item
    g_rows = ATTN_GROUP * Q_BLK
    contract_last = (((1,), (1,)), ((), ()))

    def key_start(t):
        kb = jnp.clip(Q_ROWS * t - WIN_H // 2, 0, last_kb)
        return pl.multiple_of(kb * GRID_W, GRID_W)

    def locate(g):
        item = g // groups_per_item
        local = g - item * groups_per_item
        return item, local, pl.multiple_of(local * g_rows, g_rows)

    def scores(g, slot):
        item, local, row0 = locate(g)
        q = q_ref[item, pl.ds(row0, g_rows), :]
        sc_bufs[slot][...] = lax.dot_general(q, kc_ref[item], contract_last, preferred_element_type=F32)
        for i in range(ATTN_GROUP):
            t = local * ATTN_GROUP + i
            geom = jnp.where(t < 2, t, jnp.where(t >= n_blocks - 2, t - (n_blocks - 5), 2))
            k = k_ref[item, pl.ds(key_start(t), K_BLK), :]
            s = lax.dot_general(q[i * Q_BLK:(i + 1) * Q_BLK], k, contract_last, preferred_element_type=F32)
            s_bufs[slot][i * Q_BLK:(i + 1) * Q_BLK, :] = s + bias_ref[geom]

    def softmax(slot):
        s = s_bufs[slot][...]
        sc = sc_bufs[slot][...]
        m = jnp.maximum(jnp.max(s, axis=-1, keepdims=True), jnp.max(sc, axis=-1, keepdims=True))
        p = jnp.exp2(s - m)
        pc = jnp.exp2(sc - m)
        den_bufs[slot][...] = jnp.sum(p, axis=-1, keepdims=True) + jnp.sum(pc, axis=-1, keepdims=True)
        p_bufs[slot][...] = p.astype(BF16)
        pc_bufs[slot][...] = pc.astype(BF16)

    def weighted_sum(g, slot):
        item, local, row0 = locate(g)
        o_ctx = _dot(pc_bufs[slot][...], vc_ref[item])
        outs = []
        for i in range(ATTN_GROUP):
            v = v_ref[item, pl.ds(key_start(local * ATTN_GROUP + i), K_BLK), :]
            outs.append(_dot(p_bufs[slot][i * Q_BLK:(i + 1) * Q_BLK, :], v))
        o = (jnp.concatenate(outs, axis=0) + o_ctx) / den_bufs[slot][...]
        o_ref[item, pl.ds(row0, g_rows), :] = o.astype(o_ref.dtype)

    scores(0, 0)
    scores(1, 1)
    softmax(0)

    def steady(j, carry):
        for parity in range(2):
            tau = 2 * j + 2 + parity
            scores(tau, parity)
            softmax(1 - parity)
            weighted_sum(tau - 2, parity)
        return carry

    lax.fori_loop(0, (n_groups - 2) // 2, steady, 0)
    softmax(1)
    weighted_sum(n_groups - 2, 0)
    weighted_sum(n_groups - 1, 1)


def _neighbourhood_attention(qkv, kvc, strip, cast_srcs=()):
    bsz, n, _ = qkv.shape
    ctx_len = kvc.shape[1]
    rows = n // GRID_W
    n_blocks = rows // Q_ROWS
    assert n_blocks % (2 * ATTN_GROUP) == 0 and n_blocks // ATTN_GROUP >= 4
    assert bsz % ATTN_ITEMS == 0
    n_batch_steps = bsz // ATTN_ITEMS
    jobs = [_cast_job(src, layer, N_HEADS * n_batch_steps) for src, layer in cast_srcs]
    cast_in, cast_out, cast_shapes = _cast_io(jobs, lambda h, b: h * n_batch_steps + b)
    layout = _attn_geometries(rows)
    kern = functools.partial(_attn_kernel, n_blocks=n_blocks, last_kb=rows - KEY_ROWS, n_cast=len(jobs), layout=layout)
    seq_spec = lambda off: pl.BlockSpec((ATTN_ITEMS, n, HEAD_DIM), lambda h, b: (b, 0, off + h))
    ctx_spec = lambda off: pl.BlockSpec((ATTN_ITEMS, ctx_len, HEAD_DIM), lambda h, b: (b, 0, off + h))
    g_rows = ATTN_GROUP * Q_BLK
    slot_pair = lambda cols, dtype: [pltpu.VMEM((g_rows, cols), dtype)] * 2
    return pl.pallas_call(
        kern,
        grid=(N_HEADS, n_batch_steps),
        in_specs=[
            seq_spec(0), seq_spec(N_HEADS), seq_spec(2 * N_HEADS),
            ctx_spec(0), ctx_spec(N_HEADS),
            pl.BlockSpec((1,) + strip.shape[1:], lambda h, b: (h, 0, 0)),
        ] + cast_in,
        out_specs=[pl.BlockSpec((ATTN_ITEMS, n, HEAD_DIM), lambda h, b: (b, 0, h))] + cast_out,
        out_shape=[jax.ShapeDtypeStruct((bsz, n, D_MODEL), BF16)] + cast_shapes,
        scratch_shapes=(slot_pair(K_BLK, F32) + slot_pair(ctx_len, F32) + slot_pair(K_BLK, BF16)
                        + slot_pair(ctx_len, BF16) + slot_pair(1, F32)
                        + [pltpu.VMEM((len(layout), Q_BLK, K_BLK), F32)]),
        compiler_params=_params("arbitrary", "arbitrary"),
        name="neighbourhood_attention",
    )(qkv, qkv, qkv, kvc, kvc, strip, *[job.src for job in jobs])


def _proj_res_ln_kernel(y_ref, w_ref, h_ref, g_ref, lg_ref, lb_ref, o_ref):
    sub = y_ref.shape[0] // PROJ_SUB_BLOCKS
    for r in range(PROJ_SUB_BLOCKS):
        rows = slice(r * sub, (r + 1) * sub)
        o_ref[rows, :] = ALPHA * h_ref[rows, :] + g_ref[0] * _dot(y_ref[rows, :], w_ref[...])
        _layer_norm_in_place(o_ref, rows, lg_ref[...], lb_ref[...])


def _proj_res_ln(y, w, h, gate, ln_g, ln_b, rows_per_batch, tm=1024):
    m, k = y.shape
    d = w.shape[1]
    tiles_per_batch = rows_per_batch // tm
    return pl.pallas_call(
        _proj_res_ln_kernel,
        grid=(m // tm,),
        in_specs=[
            pl.BlockSpec((tm, k), lambda i: (i, 0)),
            pl.BlockSpec((k, d), lambda i: (0, 0), pipeline_mode=pl.Buffered(1)),
            pl.BlockSpec((tm, d), lambda i: (i, 0)),
            pl.BlockSpec((1, 1, d), lambda i: (i // tiles_per_batch, 0, 0)),
            pl.BlockSpec((1, d), lambda i: (0, 0)),
            pl.BlockSpec((1, d), lambda i: (0, 0)),
        ],
        out_specs=pl.BlockSpec((tm, d), lambda i: (i, 0)),
        out_shape=jax.ShapeDtypeStruct((m, d), F32),
        compiler_params=_params("parallel"),
        name="proj_res_ln",
    )(y, w, h, gate, ln_g.reshape(1, d), ln_b.reshape(1, d))


def _ffn_kernel(h_ref, sh_ref, sc_ref, g_ref, wg_ref, wu_ref, wo_ref, lg_ref, lb_ref, o_ref, a_ref):
    f = pl.program_id(1)

    @pl.when(f == 0)
    def _():
        h = h_ref[...]
        a_ref[...] = (h * (1.0 + sc_ref[0]) + sh_ref[0]).astype(BF16)
        o_ref[...] = ALPHA * h

    a = a_ref[...]
    act = _silu(_dot(a, wg_ref[...])) * _dot(a, wu_ref[...])
    o_ref[...] += g_ref[0] * _dot(act.astype(BF16), wo_ref[...])

    @pl.when(f == pl.num_programs(1) - 1)
    def _():
        _layer_norm_in_place(o_ref, slice(0, o_ref.shape[0]), lg_ref[...], lb_ref[...])


def _ffn(h, sh, sc, gate, w_in, w_out, ln_g, ln_b, rows_per_batch, tm=1024, tf=512):
    m, d = h.shape
    d_ff = w_out.shape[0]
    n_f = d_ff // tf
    tiles_per_batch = rows_per_batch // tm
    mod_spec = pl.BlockSpec((1, 1, d), lambda i, f: (i // tiles_per_batch, 0, 0))
    vec_spec = pl.BlockSpec((1, d), lambda i, f: (0, 0))
    return pl.pallas_call(
        _ffn_kernel,
        grid=(m // tm, n_f),
        in_specs=[
            pl.BlockSpec((tm, d), lambda i, f: (i, 0)),
            mod_spec, mod_spec, mod_spec,
            pl.BlockSpec((d, tf), lambda i, f: (0, f)),
            pl.BlockSpec((d, tf), lambda i, f: (0, n_f + f)),
            pl.BlockSpec((tf, d), lambda i, f: (f, 0)),
            vec_spec, vec_spec,
        ],
        out_specs=pl.BlockSpec((tm, d), lambda i, f: (i, 0)),
        out_shape=jax.ShapeDtypeStruct((m, d), F32),
        scratch_shapes=[pltpu.VMEM((tm, d), BF16)],
        compiler_params=_params("parallel", "arbitrary"),
        name="swiglu_ffn",
    )(h, sh, sc, gate, w_in, w_in, w_out, ln_g.reshape(1, d), ln_b.reshape(1, d))


def _gmlp_kernel(*refs, n_cast):
    h_ref, sh_ref, sc_ref, wu_ref, wv_ref, lg_ref, lb_ref, ws_ref, bs_ref = refs[:9]
    o_ref = refs[9 + n_cast]
    u_ref, vf_ref, v_ref = refs[10 + 2 * n_cast:]
    _run_casts(refs[9:9 + n_cast], refs[10 + n_cast:10 + 2 * n_cast])

    def gelu(z):
        return 0.5 * z * (1.0 + lax.erf(z * (2.0 ** -0.5)))

    d = u_ref.shape[1]
    col_chunks = [slice(c * GMLP_COL_CHUNK, (c + 1) * GMLP_COL_CHUNK) for c in range(d // GMLP_COL_CHUNK)]
    for r0 in range(0, h_ref.shape[0], GMLP_SUB_ROWS):
        sub = slice(r0, r0 + GMLP_SUB_ROWS)
        a = (h_ref[sub, :] * (1.0 + sc_ref[0]) + sh_ref[0]).astype(BF16)
        for cols in col_chunks:
            vf_ref[sub, cols] = gelu(_dot(a, wv_ref[:, cols]))
        for s0 in range(r0, r0 + GMLP_SUB_ROWS, BF16_SUBLANES):
            strip = slice(s0, s0 + BF16_SUBLANES)
            v_ref[strip, :] = _layer_norm(vf_ref[strip, :], lg_ref[...], lb_ref[...]).astype(BF16)
        for cols in col_chunks:
            u_ref[sub, cols] = gelu(_dot(a, wu_ref[:, cols]))
        for c0 in range(r0, r0 + GMLP_SUB_ROWS, CHUNK):
            rows = slice(c0, c0 + CHUNK)
            for g in range(N_GROUPS):
                cols = slice(g * GROUP_DIM, (g + 1) * GROUP_DIM)
                mixed = _dot(ws_ref[g], v_ref[rows, cols]) + bs_ref[:, cols]
                o_ref[rows, cols] = (u_ref[rows, cols] * mixed).astype(o_ref.dtype)


def _gmlp_gate(h, sh, sc, w_in, ln_g, ln_b, w_s, b_s_full, rows_per_batch, tm=512, cast_srcs=()):
    m, d = h.shape
    tiles_per_batch = rows_per_batch // tm
    mod_spec = pl.BlockSpec((1, 1, d), lambda i: (i // tiles_per_batch, 0, 0))
    vec_spec = pl.BlockSpec((1, d), lambda i: (0, 0))
    jobs = [_cast_job(src, layer, m // tm) for src, layer in cast_srcs]
    cast_in, cast_out, cast_shapes = _cast_io(jobs, lambda i: i)
    return pl.pallas_call(
        functools.partial(_gmlp_kernel, n_cast=len(jobs)),
        grid=(m // tm,),
        in_specs=[
            pl.BlockSpec((tm, d), lambda i: (i, 0)),
            mod_spec, mod_spec,
            pl.BlockSpec((d, d), lambda i: (0, 0)),
            pl.BlockSpec((d, d), lambda i: (0, 1)),
            vec_spec, vec_spec,
            pl.BlockSpec((N_GROUPS, CHUNK, CHUNK), lambda i: (0, 0, 0)),
            pl.BlockSpec((CHUNK, d), lambda i: (0, 0)),
        ] + cast_in,
        out_specs=[pl.BlockSpec((tm, d), lambda i: (i, 0))] + cast_out,
        out_shape=[jax.ShapeDtypeStruct((m, d), BF16)] + cast_shapes,
        scratch_shapes=[pltpu.VMEM((tm, d), F32), pltpu.VMEM((tm, d), F32), pltpu.VMEM((tm, d), BF16)],
        compiler_params=_params("arbitrary"),
        name="gmlp_gate",
    )(h, sh, sc, w_in, w_in, ln_g.reshape(1, d), ln_b.reshape(1, d), w_s, b_s_full, *[job.src for job in jobs])


def kernel(x, c, ctx, c_ctx, ada_w, ada_b, ln_g, ln_b, na_w_qkv, na_w_o, na_rpb, gm_w_in, gm_ln_g, gm_ln_b,
           gm_w_s, gm_b_s, gm_w_out, ffn_w_in, ffn_w_out):
    bsz, n, d = x.shape
    ctx_len = ctx.shape[1]
    rows = n // GRID_W
    assert d == D_MODEL and n % (GRID_W * Q_ROWS) == 0 and rows >= KEY_ROWS + 2 * Q_ROWS

    cond = jnp.concatenate([c, c_ctx[None, :], jnp.zeros((8 - bsz - 1, d), F32)], axis=0)
    ada, w_qkv = _ada_params(cond, ada_w, ada_b, cast_srcs=[(na_w_qkv, 0)])

    def mod_vectors(layer):
        parts = jnp.split(ada[layer], 6, axis=-1)
        latent = [p[:bsz].reshape(bsz, 1, d) for p in parts]
        context = [p[bsz:bsz + 1].reshape(1, 1, d) for p in parts]
        return latent, context

    h = x.reshape(bsz * n, d)

    (sh1, sc1, g1, sh2, sc2, g2), (csh1, csc1, _, _, _, _) = mod_vectors(0)
    q_scale = jnp.concatenate([jnp.full((1, d), HEAD_DIM ** -0.5 * LOG2_E, F32), jnp.ones((1, 2 * d), F32)], axis=1)
    qkv, = _mod_matmul(h, sh1, sc1, w_qkv, q_scale, n, tm=1024, tn=2048, name="qkv_proj")
    kvc, = _mod_matmul(ctx.reshape(bsz * ctx_len, d), csh1, csc1, w_qkv, jnp.ones((1, 2 * d), F32), ctx_len,
                       tm=bsz * ctx_len, tn=1024, col_start=d, name="ctx_kv_proj")
    strip = _attn_bias_strip(na_rpb[0])
    o, w_o, gm_w_in_bf, gm_w_out_bf, ffn_w_in0, ffn_w_out0 = _neighbourhood_attention(
        qkv.reshape(bsz, n, 3 * d), kvc.reshape(bsz, ctx_len, 2 * d), strip,
        cast_srcs=[(na_w_o, 0), (gm_w_in, 0), (gm_w_out, 0), (ffn_w_in, 0), (ffn_w_out, 0)])
    h = _proj_res_ln(o.reshape(bsz * n, d), w_o, h, g1, ln_g[0, 0], ln_b[0, 0], n)
    h = _ffn(h, sh2, sc2, g2, ffn_w_in0, ffn_w_out0, ln_g[0, 1], ln_b[0, 1], n)

    (sh1, sc1, g1, sh2, sc2, g2), _ = mod_vectors(1)
    b_s_full = jnp.repeat(jnp.transpose(gm_b_s[0]), GROUP_DIM, axis=1)
    uv, ffn_w_in1, ffn_w_out1 = _gmlp_gate(h, sh1, sc1, gm_w_in_bf, gm_ln_g[0], gm_ln_b[0], gm_w_s[0].astype(BF16),
                                            b_s_full, n, cast_srcs=[(ffn_w_in, 1), (ffn_w_out, 1)])
    h = _proj_res_ln(uv, gm_w_out_bf, h, g1, ln_g[1, 0], ln_b[1, 0], n)
    h = _ffn(h, sh2, sc2, g2, ffn_w_in1, ffn_w_out1, ln_g[1, 1], ln_b[1, 1], n)
    return h.reshape(bsz, n, d)
```

```python
import functools
from typing import NamedTuple

import numpy as np
import jax
import jax.numpy as jnp
from jax import lax
from jax.experimental import pallas as pl
from jax.experimental.pallas import tpu as pltpu

D_MODEL = 2048
GRID_W = 64
N_HEADS = 16
HEAD_DIM = D_MODEL // N_HEADS
WIN_H = 8
WIN_W = 16
CHUNK = 128
N_GROUPS = 16
GROUP_DIM = D_MODEL // N_GROUPS
DEPTH = 2
ALPHA = (2 * DEPTH) ** 0.25
LN_EPS = 1e-5

VMEM_LIMIT_BYTES = 60 * 1024 * 1024
MASK_VALUE = -1e30
LOG2_E = 1.4426950408889634

Q_ROWS = 2
KEY_ROWS = Q_ROWS + WIN_H - 1
Q_BLK = Q_ROWS * GRID_W
K_BLK = KEY_ROWS * GRID_W
GMLP_COL_CHUNK = 512
GMLP_SUB_ROWS = 256
PROJ_SUB_BLOCKS = 4
ATTN_GROUP = 8
ATTN_ITEMS = 2
STRIP_PAD_TILES = 1

BF16 = jnp.bfloat16
F32 = jnp.float32
BF16_SUBLANES = 16
LN_STRIP_ROWS = 8


def _params(*sem):
    return pltpu.CompilerParams(dimension_semantics=sem, vmem_limit_bytes=VMEM_LIMIT_BYTES)


def _layer_norm(x, g, b):
    mu = jnp.mean(x, axis=-1, keepdims=True)
    xc = x - mu
    var = jnp.mean(xc * xc, axis=-1, keepdims=True)
    return xc * lax.rsqrt(var + LN_EPS) * g + b


def _layer_norm_in_place(ref, rows, g, b):
    for r0 in range(rows.start, rows.stop, LN_STRIP_ROWS):
        strip = slice(r0, r0 + LN_STRIP_ROWS)
        ref[strip, :] = _layer_norm(ref[strip, :], g, b)


def _silu(x):
    return x * (1.0 / (1.0 + jnp.exp(-x)))


def _dot(a, b):
    return jnp.dot(a, b, preferred_element_type=F32)


class _CastJob(NamedTuple):
    src: jax.Array
    layer: int
    rows: int
    steps_per_block: int


def _cast_job(src, layer, n_steps):
    r = src.shape[1]
    for steps_per_block in (1, 2, 4, 8):
        rows, rem = divmod(r * steps_per_block, n_steps)
        if rem == 0 and rows % BF16_SUBLANES == 0:
            return _CastJob(src, layer, rows, steps_per_block)
    raise ValueError(f"cannot split {r} rows over {n_steps} steps")


def _cast_io(jobs, step_of):
    in_specs, out_specs, out_shapes = [], [], []
    for job in jobs:
        _, r, c = job.src.shape
        in_specs.append(pl.BlockSpec(
            (None, job.rows, c), lambda *g, job=job: (job.layer, step_of(*g) // job.steps_per_block, 0)))
        out_specs.append(pl.BlockSpec(
            (job.rows, c), lambda *g, job=job: (step_of(*g) // job.steps_per_block, 0)))
        out_shapes.append(jax.ShapeDtypeStruct((r, c), BF16))
    return in_specs, out_specs, out_shapes


def _run_casts(src_refs, dst_refs):
    for src, dst in zip(src_refs, dst_refs):
        dst[...] = src[...].astype(dst.dtype)


def _ada_kernel(*refs, n_cast):
    cond_ref, w_ref, b_ref = refs[:3]
    o_ref = refs[3 + n_cast]
    s = _silu(cond_ref[...]).astype(BF16)
    o_ref[0] = _dot(s, w_ref[0].astype(BF16)) + b_ref[0]
    _run_casts(refs[3:3 + n_cast], refs[4 + n_cast:4 + 2 * n_cast])


def _ada_params(cond, ada_w, ada_b, tn=768, cast_srcs=()):
    depth, d, n6 = ada_w.shape
    rows = cond.shape[0]
    grid = (depth, n6 // tn)
    jobs = [_cast_job(src, layer, grid[0] * grid[1]) for src, layer in cast_srcs]
    cast_in, cast_out, cast_shapes = _cast_io(jobs, lambda l, j: l * grid[1] + j)
    return pl.pallas_call(
        functools.partial(_ada_kernel, n_cast=len(jobs)),
        grid=grid,
        in_specs=[
            pl.BlockSpec((rows, d), lambda l, j: (0, 0)),
            pl.BlockSpec((1, d, tn), lambda l, j: (l, 0, j)),
            pl.BlockSpec((1, 1, tn), lambda l, j: (l, 0, j)),
        ] + cast_in,
        out_specs=[pl.BlockSpec((1, rows, tn), lambda l, j: (l, 0, j))] + cast_out,
        out_shape=[jax.ShapeDtypeStruct((depth, rows, n6), F32)] + cast_shapes,
        compiler_params=_params("arbitrary", "arbitrary"),
        name="ada_params",
    )(cond, ada_w, ada_b.reshape(depth, 1, n6), *[job.src for job in jobs])


def _mod_matmul_kernel(*refs, n_cast):
    x_ref, sh_ref, sc_ref, w_ref, cs_ref = refs[:5]
    cast_src = refs[5:5 + n_cast]
    o_ref = refs[5 + n_cast]
    cast_dst = refs[6 + n_cast:6 + 2 * n_cast]
    a_ref = refs[6 + 2 * n_cast]

    @pl.when(pl.program_id(1) == 0)
    def _():
        a_ref[...] = (x_ref[...] * (1.0 + sc_ref[0]) + sh_ref[0]).astype(BF16)

    o_ref[...] = (_dot(a_ref[...], w_ref[...]) * cs_ref[...]).astype(o_ref.dtype)
    _run_casts(cast_src, cast_dst)


def _mod_matmul(x, sh, sc, w, col_scale, rows_per_batch, tm, tn, col_start=0, cast_srcs=(), name="mod_matmul"):
    m, d = x.shape
    n = w.shape[1] - col_start
    col_blk = col_start // tn
    tiles_per_batch = rows_per_batch // tm
    grid = (m // tm, n // tn)
    jobs = [_cast_job(src, layer, grid[0] * grid[1]) for src, layer in cast_srcs]
    cast_in, cast_out, cast_shapes = _cast_io(jobs, lambda i, j: i * grid[1] + j)
    if sh.shape[0] == 1:
        mod_idx = lambda i, j: (0, 0, 0)
    else:
        mod_idx = lambda i, j: (i // tiles_per_batch, 0, 0)
    return pl.pallas_call(
        functools.partial(_mod_matmul_kernel, n_cast=len(jobs)),
        grid=grid,
        in_specs=[
            pl.BlockSpec((tm, d), lambda i, j: (i, 0)),
            pl.BlockSpec((1, 1, d), mod_idx),
            pl.BlockSpec((1, 1, d), mod_idx),
            pl.BlockSpec((d, tn), lambda i, j: (0, col_blk + j)),
            pl.BlockSpec((1, tn), lambda i, j: (0, j)),
        ] + cast_in,
        out_specs=[pl.BlockSpec((tm, tn), lambda i, j: (i, j))] + cast_out,
        out_shape=[jax.ShapeDtypeStruct((m, n), BF16)] + cast_shapes,
        scratch_shapes=[pltpu.VMEM((tm, d), BF16)],
        compiler_params=_params("parallel", "arbitrary"),
        name=name,
    )(x, sh, sc, w, col_scale, *[job.src for job in jobs])


def _attn_geometries(rows):
    n_blocks = rows // Q_ROWS
    last_kb = rows - KEY_ROWS
    geoms = [(0, 0), (Q_ROWS, 0), (2 * Q_ROWS, 0),
             ((n_blocks - 2) * Q_ROWS, last_kb), ((n_blocks - 1) * Q_ROWS, last_kb)]
    layout = []
    for rb, kb in geoms:
        per_row = []
        for dr in range(Q_ROWS):
            r = rb + dr
            r0 = min(max(r - WIN_H // 2, 0), rows - WIN_H)
            first_tile = kb - r + (WIN_H - 1) + STRIP_PAD_TILES
            assert 0 <= first_tile and first_tile + KEY_ROWS <= 2 * WIN_H - 1 + 2 * STRIP_PAD_TILES
            per_row.append((first_tile, max(r0 - kb, 0), min(r0 - kb + WIN_H, KEY_ROWS)))
        layout.append(per_row)
    return layout


def _attn_bias_strip(rpb):
    n_heads, _, n_col_rel = rpb.shape
    c = np.arange(GRID_W)[:, None]
    kc = np.arange(GRID_W)[None, :]
    c0 = np.clip(c - WIN_W // 2, 0, GRID_W - WIN_W)
    col_ok = (kc >= c0) & (kc < c0 + WIN_W)
    select = (kc - c + WIN_W - 1)[None] == np.arange(n_col_rel)[:, None, None]
    select = jnp.asarray((select & col_ok[None]).astype(np.float32))
    tiles = jnp.einsum("hab,bck->hack", rpb * LOG2_E, select, precision=lax.Precision.HIGHEST)
    tiles = jnp.where(col_ok[None, None], tiles, MASK_VALUE)
    masked = jnp.full((n_heads, STRIP_PAD_TILES, GRID_W, GRID_W), MASK_VALUE, rpb.dtype)
    return jnp.concatenate([masked, tiles, masked], axis=1)


def _attn_kernel(*refs, n_blocks, last_kb, n_cast, layout):
    q_ref, k_ref, v_ref, kc_ref, vc_ref, strip_ref = refs[:6]
    o_ref = refs[6 + n_cast]
    s0, s1, sc0, sc1, p0, p1, pc0, pc1, den0, den1, bias_ref = refs[7 + 2 * n_cast:]

    @pl.when(pl.program_id(1) == 0)
    def _():
        key_row = lax.broadcasted_iota(jnp.int32, (GRID_W, K_BLK), 1) // GRID_W
        for geom, per_row in enumerate(layout):
            for dr, (first_tile, lo, hi) in enumerate(per_row):
                window = jnp.concatenate([strip_ref[0, first_tile + i] for i in range(KEY_ROWS)], axis=-1)
                valid = (key_row >= lo) & (key_row < hi)
                bias_ref[geom, dr * GRID_W:(dr + 1) * GRID_W, :] = jnp.where(valid, window, MASK_VALUE)

    _attn_pipeline(q_ref, k_ref, v_ref, kc_ref, vc_ref, bias_ref, o_ref,
                   s0, s1, sc0, sc1, p0, p1, pc0, pc1, den0, den1, n_blocks=n_blocks, last_kb=last_kb)
    _run_casts(refs[6:6 + n_cast], refs[7 + n_cast:7 + 2 * n_cast])


def _attn_pipeline(q_ref, k_ref, v_ref, kc_ref, vc_ref, bias_ref, o_ref,
                   s0, s1, sc0, sc1, p0, p1, pc0, pc1, den0, den1, *, n_blocks, last_kb):
    s_bufs, sc_bufs, p_bufs, pc_bufs, den_bufs = (s0, s1), (sc0, sc1), (p0, p1), (pc0, pc1), (den0, den1)
    groups_per_item = n_blocks // ATTN_GROUP
    n_groups = q_ref.shape[0] * groups_per_item
    g_rows = ATTN_GROUP * Q_BLK
    contract_last = (((1,), (1,)), ((), ()))

    def key_start(t):
        kb = jnp.clip(Q_ROWS * t - WIN_H // 2, 0, last_kb)
        return pl.multiple_of(kb * GRID_W, GRID_W)

    def locate(g):
        item = g // groups_per_item
        local = g - item * groups_per_item
        return item, local, pl.multiple_of(local * g_rows, g_rows)

    def scores(g, slot):
        item, local, row0 = locate(g)
        q = q_ref[item, pl.ds(row0, g_rows), :]
        sc_bufs[slot][...] = lax.dot_general(q, kc_ref[item], contract_last, preferred_element_type=F32)
        for i in range(ATTN_GROUP):
            t = local * ATTN_GROUP + i
            geom = jnp.where(t < 2, t, jnp.where(t >= n_blocks - 2, t - (n_blocks - 5), 2))
            k = k_ref[item, pl.ds(key_start(t), K_BLK), :]
            s = lax.dot_general(q[i * Q_BLK:(i + 1) * Q_BLK], k, contract_last, preferred_element_type=F32)
            s_bufs[slot][i * Q_BLK:(i + 1) * Q_BLK, :] = s + bias_ref[geom]

    def softmax(slot):
        s = s_bufs[slot][...]
        sc = sc_bufs[slot][...]
        m = jnp.maximum(jnp.max(s, axis=-1, keepdims=True), jnp.max(sc, axis=-1, keepdims=True))
        p = jnp.exp2(s - m)
        pc = jnp.exp2(sc - m)
        den_bufs[slot][...] = jnp.sum(p, axis=-1, keepdims=True) + jnp.sum(pc, axis=-1, keepdims=True)
        p_bufs[slot][...] = p.astype(BF16)
        pc_bufs[slot][...] = pc.astype(BF16)

    def weighted_sum(g, slot):
        item, local, row0 = locate(g)
        o_ctx = _dot(pc_bufs[slot][...], vc_ref[item])
        outs = []
        for i in range(ATTN_GROUP):
            v = v_ref[item, pl.ds(key_start(local * ATTN_GROUP + i), K_BLK), :]
            outs.append(_dot(p_bufs[slot][i * Q_BLK:(i + 1) * Q_BLK, :], v))
        o = (jnp.concatenate(outs, axis=0) + o_ctx) / den_bufs[slot][...]
        o_ref[item, pl.ds(row0, g_rows), :] = o.astype(o_ref.dtype)

    scores(0, 0)
    scores(1, 1)
    softmax(0)

    def steady(j, carry):
        for parity in range(2):
            tau = 2 * j + 2 + parity
            scores(tau, parity)
            softmax(1 - parity)
            weighted_sum(tau - 2, parity)
        return carry

    lax.fori_loop(0, (n_groups - 2) // 2, steady, 0)
    softmax(1)
    weighted_sum(n_groups - 2, 0)
    weighted_sum(n_groups - 1, 1)


def _neighbourhood_attention(qkv, kvc, strip, cast_srcs=()):
    bsz, n, _ = qkv.shape
    ctx_len = kvc.shape[1]
    rows = n // GRID_W
    n_blocks = rows // Q_ROWS
    assert bsz % ATTN_ITEMS == 0 and n_blocks % ATTN_GROUP == 0
    n_groups = ATTN_ITEMS * (n_blocks // ATTN_GROUP)
    assert n_groups % 2 == 0 and n_groups >= 4
    n_batch_steps = bsz // ATTN_ITEMS
    jobs = [_cast_job(src, layer, N_HEADS * n_batch_steps) for src, layer in cast_srcs]
    cast_in, cast_out, cast_shapes = _cast_io(jobs, lambda h, b: h * n_batch_steps + b)
    layout = _attn_geometries(rows)
    kern = functools.partial(_attn_kernel, n_blocks=n_blocks, last_kb=rows - KEY_ROWS, n_cast=len(jobs), layout=layout)
    seq_spec = lambda off: pl.BlockSpec((ATTN_ITEMS, n, HEAD_DIM), lambda h, b: (b, 0, off + h))
    ctx_spec = lambda off: pl.BlockSpec((ATTN_ITEMS, ctx_len, HEAD_DIM), lambda h, b: (b, 0, off + h))
    g_rows = ATTN_GROUP * Q_BLK
    slot_pair = lambda cols, dtype: [pltpu.VMEM((g_rows, cols), dtype)] * 2
    return pl.pallas_call(
        kern,
        grid=(N_HEADS, n_batch_steps),
        in_specs=[
            seq_spec(0), seq_spec(N_HEADS), seq_spec(2 * N_HEADS),
            ctx_spec(0), ctx_spec(N_HEADS),
            pl.BlockSpec((1,) + strip.shape[1:], lambda h, b: (h, 0, 0, 0)),
        ] + cast_in,
        out_specs=[pl.BlockSpec((ATTN_ITEMS, n, HEAD_DIM), lambda h, b: (b, 0, h))] + cast_out,
        out_shape=[jax.ShapeDtypeStruct((bsz, n, D_MODEL), BF16)] + cast_shapes,
        scratch_shapes=(slot_pair(K_BLK, F32) + slot_pair(ctx_len, F32) + slot_pair(K_BLK, BF16)
                        + slot_pair(ctx_len, BF16) + slot_pair(1, F32)
                        + [pltpu.VMEM((len(layout), Q_BLK, K_BLK), F32)]),
        compiler_params=_params("arbitrary", "arbitrary"),
        name="neighbourhood_attention",
    )(qkv, qkv, qkv, kvc, kvc, strip, *[job.src for job in jobs])


def _proj_res_ln_kernel(y_ref, w_ref, h_ref, g_ref, lg_ref, lb_ref, o_ref):
    sub = y_ref.shape[0] // PROJ_SUB_BLOCKS
    for r in range(PROJ_SUB_BLOCKS):
        rows = slice(r * sub, (r + 1) * sub)
        o_ref[rows, :] = ALPHA * h_ref[rows, :] + g_ref[0] * _dot(y_ref[rows, :], w_ref[...])
        _layer_norm_in_place(o_ref, rows, lg_ref[...], lb_ref[...])


def _proj_res_ln(y, w, h, gate, ln_g, ln_b, rows_per_batch, tm=1024):
    m, k = y.shape
    d = w.shape[1]
    tiles_per_batch = rows_per_batch // tm
    return pl.pallas_call(
        _proj_res_ln_kernel,
        grid=(m // tm,),
        in_specs=[
            pl.BlockSpec((tm, k), lambda i: (i, 0)),
            pl.BlockSpec((k, d), lambda i: (0, 0), pipeline_mode=pl.Buffered(1)),
            pl.BlockSpec((tm, d), lambda i: (i, 0)),
            pl.BlockSpec((1, 1, d), lambda i: (i // tiles_per_batch, 0, 0)),
            pl.BlockSpec((1, d), lambda i: (0, 0)),
            pl.BlockSpec((1, d), lambda i: (0, 0)),
        ],
        out_specs=pl.BlockSpec((tm, d), lambda i: (i, 0)),
        out_shape=jax.ShapeDtypeStruct((m, d), F32),
        compiler_params=_params("parallel"),
        name="proj_res_ln",
    )(y, w, h, gate, ln_g.reshape(1, d), ln_b.reshape(1, d))


def _ffn_kernel(h_ref, sh_ref, sc_ref, g_ref, w_in_hbm, w_out_hbm, lg_ref, lb_ref, o_ref,
                a_ref, wg_buf, wu_buf, wo_buf, sem, *, n_f, tf):
    i = pl.program_id(0)
    n_tiles = pl.num_programs(0)
    d_ff = n_f * tf

    def chunk_copies(f, slot):
        c0 = pl.multiple_of(f * tf, tf)
        return (
            pltpu.make_async_copy(w_in_hbm.at[:, pl.ds(c0, tf)], wg_buf.at[slot], sem.at[slot, 0]),
            pltpu.make_async_copy(w_in_hbm.at[:, pl.ds(d_ff + c0, tf)], wu_buf.at[slot], sem.at[slot, 1]),
            pltpu.make_async_copy(w_out_hbm.at[pl.ds(c0, tf), :], wo_buf.at[slot], sem.at[slot, 2]),
        )

    @pl.when(i == 0)
    def _():
        for copy in chunk_copies(0, 0):
            copy.start()

    h = h_ref[...]
    a_ref[...] = (h * (1.0 + sc_ref[0]) + sh_ref[0]).astype(BF16)
    o_ref[...] = ALPHA * h

    def chunk(f, carry):
        slot = (i * n_f + f) % 2
        for copy in chunk_copies(f, slot):
            copy.wait()

        @pl.when((f + 1 < n_f) | (i + 1 < n_tiles))
        def _():
            for copy in chunk_copies((f + 1) % n_f, 1 - slot):
                copy.start()

        a = a_ref[...]
        act = _silu(_dot(a, wg_buf[slot])) * _dot(a, wu_buf[slot])
        o_ref[...] += g_ref[0] * _dot(act.astype(BF16), wo_buf[slot])
        return carry

    lax.fori_loop(0, n_f, chunk, 0)
    _layer_norm_in_place(o_ref, slice(0, o_ref.shape[0]), lg_ref[...], lb_ref[...])


def _ffn(h, sh, sc, gate, w_in, w_out, ln_g, ln_b, rows_per_batch, tm=1024, tf=512):
    m, d = h.shape
    d_ff = w_out.shape[0]
    n_f = d_ff // tf
    tiles_per_batch = rows_per_batch // tm
    mod_spec = pl.BlockSpec((1, 1, d), lambda i: (i // tiles_per_batch, 0, 0))
    vec_spec = pl.BlockSpec((1, d), lambda i: (0, 0))
    hbm_spec = pl.BlockSpec(memory_space=pl.ANY)
    return pl.pallas_call(
        functools.partial(_ffn_kernel, n_f=n_f, tf=tf),
        grid=(m // tm,),
        in_specs=[
            pl.BlockSpec((tm, d), lambda i: (i, 0)),
            mod_spec, mod_spec, mod_spec,
            hbm_spec, hbm_spec,
            vec_spec, vec_spec,
        ],
        out_specs=pl.BlockSpec((tm, d), lambda i: (i, 0)),
        out_shape=jax.ShapeDtypeStruct((m, d), F32),
        scratch_shapes=[
            pltpu.VMEM((tm, d), BF16),
            pltpu.VMEM((2, d, tf), BF16), pltpu.VMEM((2, d, tf), BF16), pltpu.VMEM((2, tf, d), BF16),
            pltpu.SemaphoreType.DMA((2, 3)),
        ],
        compiler_params=_params("arbitrary"),
        name="swiglu_ffn",
    )(h, sh, sc, gate, w_in, w_out, ln_g.reshape(1, d), ln_b.reshape(1, d))


def _gmlp_kernel(*refs, n_cast):
    h_ref, sh_ref, sc_ref, wu_ref, wv_ref, lg_ref, lb_ref, ws_ref, bs_ref = refs[:9]
    o_ref = refs[9 + n_cast]
    u_ref, vf_ref, v_ref = refs[10 + 2 * n_cast:]
    _run_casts(refs[9:9 + n_cast], refs[10 + n_cast:10 + 2 * n_cast])

    def gelu(z):
        return 0.5 * z * (1.0 + lax.erf(z * (2.0 ** -0.5)))

    d = u_ref.shape[1]
    col_chunks = [slice(c * GMLP_COL_CHUNK, (c + 1) * GMLP_COL_CHUNK) for c in range(d // GMLP_COL_CHUNK)]
    for r0 in range(0, h_ref.shape[0], GMLP_SUB_ROWS):
        sub = slice(r0, r0 + GMLP_SUB_ROWS)
        a = (h_ref[sub, :] * (1.0 + sc_ref[0]) + sh_ref[0]).astype(BF16)
        for cols in col_chunks:
            vf_ref[sub, cols] = gelu(_dot(a, wv_ref[:, cols]))
        for s0 in range(r0, r0 + GMLP_SUB_ROWS, BF16_SUBLANES):
            strip = slice(s0, s0 + BF16_SUBLANES)
            v_ref[strip, :] = _layer_norm(vf_ref[strip, :], lg_ref[...], lb_ref[...]).astype(BF16)
        for cols in col_chunks:
            u_ref[sub, cols] = gelu(_dot(a, wu_ref[:, cols]))
        for c0 in range(r0, r0 + GMLP_SUB_ROWS, CHUNK):
            rows = slice(c0, c0 + CHUNK)
            for g in range(N_GROUPS):
                cols = slice(g * GROUP_DIM, (g + 1) * GROUP_DIM)
                mixed = _dot(ws_ref[g], v_ref[rows, cols]) + bs_ref[:, cols]
                o_ref[rows, cols] = (u_ref[rows, cols] * mixed).astype(o_ref.dtype)


def _gmlp_gate(h, sh, sc, w_in, ln_g, ln_b, w_s, b_s_full, rows_per_batch, tm=512, cast_srcs=()):
    m, d = h.shape
    tiles_per_batch = rows_per_batch // tm
    mod_spec = pl.BlockSpec((1, 1, d), lambda i: (i // tiles_per_batch, 0, 0))
    vec_spec = pl.BlockSpec((1, d), lambda i: (0, 0))
    jobs = [_cast_job(src, layer, m // tm) for src, layer in cast_srcs]
    cast_in, cast_out, cast_shapes = _cast_io(jobs, lambda i: i)
    return pl.pallas_call(
        functools.partial(_gmlp_kernel, n_cast=len(jobs)),
        grid=(m // tm,),
        in_specs=[
            pl.BlockSpec((tm, d), lambda i: (i, 0)),
            mod_spec, mod_spec,
            pl.BlockSpec((d, d), lambda i: (0, 0)),
            pl.BlockSpec((d, d), lambda i: (0, 1)),
            vec_spec, vec_spec,
            pl.BlockSpec((N_GROUPS, CHUNK, CHUNK), lambda i: (0, 0, 0)),
            pl.BlockSpec((CHUNK, d), lambda i: (0, 0)),
        ] + cast_in,
        out_specs=[pl.BlockSpec((tm, d), lambda i: (i, 0))] + cast_out,
        out_shape=[jax.ShapeDtypeStruct((m, d), BF16)] + cast_shapes,
        scratch_shapes=[pltpu.VMEM((tm, d), F32), pltpu.VMEM((tm, d), F32), pltpu.VMEM((tm, d), BF16)],
        compiler_params=_params("arbitrary"),
        name="gmlp_gate",
    )(h, sh, sc, w_in, w_in, ln_g.reshape(1, d), ln_b.reshape(1, d), w_s, b_s_full, *[job.src for job in jobs])


def kernel(x, c, ctx, c_ctx, ada_w, ada_b, ln_g, ln_b, na_w_qkv, na_w_o, na_rpb, gm_w_in, gm_ln_g, gm_ln_b,
           gm_w_s, gm_b_s, gm_w_out, ffn_w_in, ffn_w_out):
    bsz, n, d = x.shape
    ctx_len = ctx.shape[1]
    rows = n // GRID_W
    assert d == D_MODEL and n % (GRID_W * Q_ROWS) == 0 and rows >= KEY_ROWS + 2 * Q_ROWS

    cond = jnp.concatenate([c, c_ctx[None, :], jnp.zeros((8 - bsz - 1, d), F32)], axis=0)
    ada, w_qkv = _ada_params(cond, ada_w, ada_b, cast_srcs=[(na_w_qkv, 0)])

    def mod_vectors(layer):
        parts = jnp.split(ada[layer], 6, axis=-1)
        latent = [p[:bsz].reshape(bsz, 1, d) for p in parts]
        context = [p[bsz:bsz + 1].reshape(1, 1, d) for p in parts]
        return latent, context

    h = x.reshape(bsz * n, d)

    (sh1, sc1, g1, sh2, sc2, g2), (csh1, csc1, _, _, _, _) = mod_vectors(0)
    q_scale = jnp.concatenate([jnp.full((1, d), HEAD_DIM ** -0.5 * LOG2_E, F32), jnp.ones((1, 2 * d), F32)], axis=1)
    qkv, = _mod_matmul(h, sh1, sc1, w_qkv, q_scale, n, tm=1024, tn=2048, name="qkv_proj")
    kvc, = _mod_matmul(ctx.reshape(bsz * ctx_len, d), csh1, csc1, w_qkv, jnp.ones((1, 2 * d), F32), ctx_len,
                       tm=bsz * ctx_len, tn=1024, col_start=d, name="ctx_kv_proj")
    strip = _attn_bias_strip(na_rpb[0])
    o, w_o, gm_w_in_bf, gm_w_out_bf, ffn_w_in0, ffn_w_out0 = _neighbourhood_attention(
        qkv.reshape(bsz, n, 3 * d), kvc.reshape(bsz, ctx_len, 2 * d), strip,
        cast_srcs=[(na_w_o, 0), (gm_w_in, 0), (gm_w_out, 0), (ffn_w_in, 0), (ffn_w_out, 0)])
    h = _proj_res_ln(o.reshape(bsz * n, d), w_o, h, g1, ln_g[0, 0], ln_b[0, 0], n)
    h = _ffn(h, sh2, sc2, g2, ffn_w_in0, ffn_w_out0, ln_g[0, 1], ln_b[0, 1], n)

    (sh1, sc1, g1, sh2, sc2, g2), _ = mod_vectors(1)
    b_s_full = jnp.repeat(jnp.transpose(gm_b_s[0]), GROUP_DIM, axis=1)
    uv, ffn_w_in1, ffn_w_out1 = _gmlp_gate(h, sh1, sc1, gm_w_in_bf, gm_ln_g[0], gm_ln_b[0], gm_w_s[0].astype(BF16),
                                            b_s_full, n, cast_srcs=[(ffn_w_in, 1), (ffn_w_out, 1)])
    h = _proj_res_ln(uv, gm_w_out_bf, h, g1, ln_g[1, 0], ln_b[1, 0], n)
    h = _ffn(h, sh2, sc2, g2, ffn_w_in1, ffn_w_out1, ln_g[1, 1], ln_b[1, 1], n)
    return h.reshape(bsz, n, d)
```

```python
import functools
from typing import NamedTuple

import numpy as np
import jax
import jax.numpy as jnp
from jax import lax
from jax.experimental import pallas as pl
from jax.experimental.pallas import tpu as pltpu

D_MODEL = 2048
GRID_W = 64
N_HEADS = 16
HEAD_DIM = D_MODEL // N_HEADS
WIN_H = 8
WIN_W = 16
CHUNK = 128
N_GROUPS = 16
GROUP_DIM = D_MODEL // N_GROUPS
DEPTH = 2
ALPHA = (2 * DEPTH) ** 0.25
LN_EPS = 1e-5

VMEM_LIMIT_BYTES = 60 * 1024 * 1024
FFN_VMEM_LIMIT_BYTES = 63 * 1024 * 1024
MASK_VALUE = -1e30
LOG2_E = 1.4426950408889634

Q_ROWS = 2
KEY_ROWS = Q_ROWS + WIN_H - 1
Q_BLK = Q_ROWS * GRID_W
K_BLK = KEY_ROWS * GRID_W
GMLP_COL_CHUNK = 512
GMLP_SUB_ROWS = 256
PROJ_SUB_BLOCKS = 4
ATTN_GROUP = 8
ATTN_ITEMS = 2
STRIP_PAD_TILES = 1

BF16 = jnp.bfloat16
F32 = jnp.float32
BF16_SUBLANES = 16
LN_STRIP_ROWS = 8
FFN_PIECE_ROWS = 96


def _params(*sem, vmem_limit_bytes=VMEM_LIMIT_BYTES):
    return pltpu.CompilerParams(dimension_semantics=sem, vmem_limit_bytes=vmem_limit_bytes)


def _layer_norm(x, g, b):
    mu = jnp.mean(x, axis=-1, keepdims=True)
    xc = x - mu
    var = jnp.mean(xc * xc, axis=-1, keepdims=True)
    return xc * lax.rsqrt(var + LN_EPS) * g + b


def _layer_norm_in_place(ref, rows, g, b):
    for r0 in range(rows.start, rows.stop, LN_STRIP_ROWS):
        strip = slice(r0, r0 + LN_STRIP_ROWS)
        ref[strip, :] = _layer_norm(ref[strip, :], g, b)


def _silu(x):
    return x * (1.0 / (1.0 + jnp.exp(-x)))


def _dot(a, b):
    return jnp.dot(a, b, preferred_element_type=F32)


class _CastJob(NamedTuple):
    src: jax.Array
    layer: int
    rows: int
    steps_per_block: int


def _cast_job(src, layer, n_steps):
    r = src.shape[1]
    for steps_per_block in (1, 2, 4, 8):
        rows, rem = divmod(r * steps_per_block, n_steps)
        if rem == 0 and rows % BF16_SUBLANES == 0:
            return _CastJob(src, layer, rows, steps_per_block)
    raise ValueError(f"cannot split {r} rows over {n_steps} steps")


def _cast_io(jobs, step_of):
    in_specs, out_specs, out_shapes = [], [], []
    for job in jobs:
        _, r, c = job.src.shape
        in_specs.append(pl.BlockSpec(
            (None, job.rows, c), lambda *g, job=job: (job.layer, step_of(*g) // job.steps_per_block, 0)))
        out_specs.append(pl.BlockSpec(
            (job.rows, c), lambda *g, job=job: (step_of(*g) // job.steps_per_block, 0)))
        out_shapes.append(jax.ShapeDtypeStruct((r, c), BF16))
    return in_specs, out_specs, out_shapes


def _run_casts(src_refs, dst_refs):
    for src, dst in zip(src_refs, dst_refs):
        dst[...] = src[...].astype(dst.dtype)


def _ada_kernel(*refs, n_cast):
    cond_ref, w_ref, b_ref = refs[:3]
    o_ref = refs[3 + n_cast]
    s = _silu(cond_ref[...]).astype(BF16)
    o_ref[0] = _dot(s, w_ref[0].astype(BF16)) + b_ref[0]
    _run_casts(refs[3:3 + n_cast], refs[4 + n_cast:4 + 2 * n_cast])


def _ada_params(cond, ada_w, ada_b, tn=768, cast_srcs=()):
    depth, d, n6 = ada_w.shape
    rows = cond.shape[0]
    grid = (depth, n6 // tn)
    jobs = [_cast_job(src, layer, grid[0] * grid[1]) for src, layer in cast_srcs]
    cast_in, cast_out, cast_shapes = _cast_io(jobs, lambda l, j: l * grid[1] + j)
    return pl.pallas_call(
        functools.partial(_ada_kernel, n_cast=len(jobs)),
        grid=grid,
        in_specs=[
            pl.BlockSpec((rows, d), lambda l, j: (0, 0)),
            pl.BlockSpec((1, d, tn), lambda l, j: (l, 0, j)),
            pl.BlockSpec((1, 1, tn), lambda l, j: (l, 0, j)),
        ] + cast_in,
        out_specs=[pl.BlockSpec((1, rows, tn), lambda l, j: (l, 0, j))] + cast_out,
        out_shape=[jax.ShapeDtypeStruct((depth, rows, n6), F32)] + cast_shapes,
        compiler_params=_params("arbitrary", "arbitrary"),
        name="ada_params",
    )(cond, ada_w, ada_b.reshape(depth, 1, n6), *[job.src for job in jobs])


def _mod_matmul_kernel(*refs, n_cast):
    x_ref, sh_ref, sc_ref, w_ref, cs_ref = refs[:5]
    cast_src = refs[5:5 + n_cast]
    o_ref = refs[5 + n_cast]
    cast_dst = refs[6 + n_cast:6 + 2 * n_cast]
    a_ref = refs[6 + 2 * n_cast]

    @pl.when(pl.program_id(1) == 0)
    def _():
        a_ref[...] = (x_ref[...] * (1.0 + sc_ref[0]) + sh_ref[0]).astype(BF16)

    o_ref[...] = (_dot(a_ref[...], w_ref[...]) * cs_ref[...]).astype(o_ref.dtype)
    _run_casts(cast_src, cast_dst)


def _mod_matmul(x, sh, sc, w, col_scale, rows_per_batch, tm, tn, col_start=0, cast_srcs=(), name="mod_matmul"):
    m, d = x.shape
    n = w.shape[1] - col_start
    col_blk = col_start // tn
    tiles_per_batch = rows_per_batch // tm
    grid = (m // tm, n // tn)
    jobs = [_cast_job(src, layer, grid[0] * grid[1]) for src, layer in cast_srcs]
    cast_in, cast_out, cast_shapes = _cast_io(jobs, lambda i, j: i * grid[1] + j)
    if sh.shape[0] == 1:
        mod_idx = lambda i, j: (0, 0, 0)
    else:
        mod_idx = lambda i, j: (i // tiles_per_batch, 0, 0)
    return pl.pallas_call(
        functools.partial(_mod_matmul_kernel, n_cast=len(jobs)),
        grid=grid,
        in_specs=[
            pl.BlockSpec((tm, d), lambda i, j: (i, 0)),
            pl.BlockSpec((1, 1, d), mod_idx),
            pl.BlockSpec((1, 1, d), mod_idx),
            pl.BlockSpec((d, tn), lambda i, j: (0, col_blk + j)),
            pl.BlockSpec((1, tn), lambda i, j: (0, j)),
        ] + cast_in,
        out_specs=[pl.BlockSpec((tm, tn), lambda i, j: (i, j))] + cast_out,
        out_shape=[jax.ShapeDtypeStruct((m, n), BF16)] + cast_shapes,
        scratch_shapes=[pltpu.VMEM((tm, d), BF16)],
        compiler_params=_params("parallel", "arbitrary"),
        name=name,
    )(x, sh, sc, w, col_scale, *[job.src for job in jobs])


def _attn_geometries(rows):
    n_blocks = rows // Q_ROWS
    last_kb = rows - KEY_ROWS
    geoms = [(0, 0), (Q_ROWS, 0), (2 * Q_ROWS, 0),
             ((n_blocks - 2) * Q_ROWS, last_kb), ((n_blocks - 1) * Q_ROWS, last_kb)]
    layout = []
    for rb, kb in geoms:
        per_row = []
        for dr in range(Q_ROWS):
            r = rb + dr
            r0 = min(max(r - WIN_H // 2, 0), rows - WIN_H)
            first_tile = kb - r + (WIN_H - 1) + STRIP_PAD_TILES
            assert 0 <= first_tile and first_tile + KEY_ROWS <= 2 * WIN_H - 1 + 2 * STRIP_PAD_TILES
            per_row.append((first_tile, max(r0 - kb, 0), min(r0 - kb + WIN_H, KEY_ROWS)))
        layout.append(per_row)
    return layout


def _attn_bias_strip(rpb):
    n_heads, _, n_col_rel = rpb.shape
    c = np.arange(GRID_W)[:, None]
    kc = np.arange(GRID_W)[None, :]
    c0 = np.clip(c - WIN_W // 2, 0, GRID_W - WIN_W)
    col_ok = (kc >= c0) & (kc < c0 + WIN_W)
    select = (kc - c + WIN_W - 1)[None] == np.arange(n_col_rel)[:, None, None]
    select = jnp.asarray((select & col_ok[None]).astype(np.float32))
    tiles = jnp.einsum("hab,bck->hack", rpb * LOG2_E, select, precision=lax.Precision.HIGHEST)
    tiles = jnp.where(col_ok[None, None], tiles, MASK_VALUE)
    masked = jnp.full((n_heads, STRIP_PAD_TILES, GRID_W, GRID_W), MASK_VALUE, rpb.dtype)
    return jnp.concatenate([masked, tiles, masked], axis=1)


def _attn_kernel(*refs, n_blocks, last_kb, n_cast, layout):
    q_ref, k_ref, v_ref, kc_ref, vc_ref, strip_ref = refs[:6]
    o_ref = refs[6 + n_cast]
    s0, s1, sc0, sc1, p0, p1, pc0, pc1, den0, den1, bias_ref = refs[7 + 2 * n_cast:]

    @pl.when(pl.program_id(1) == 0)
    def _():
        key_row = lax.broadcasted_iota(jnp.int32, (GRID_W, K_BLK), 1) // GRID_W
        for geom, per_row in enumerate(layout):
            for dr, (first_tile, lo, hi) in enumerate(per_row):
                window = jnp.concatenate([strip_ref[0, first_tile + i] for i in range(KEY_ROWS)], axis=-1)
                valid = (key_row >= lo) & (key_row < hi)
                bias_ref[geom, dr * GRID_W:(dr + 1) * GRID_W, :] = jnp.where(valid, window, MASK_VALUE)

    _attn_pipeline(q_ref, k_ref, v_ref, kc_ref, vc_ref, bias_ref, o_ref,
                   s0, s1, sc0, sc1, p0, p1, pc0, pc1, den0, den1, n_blocks=n_blocks, last_kb=last_kb)
    _run_casts(refs[6:6 + n_cast], refs[7 + n_cast:7 + 2 * n_cast])


def _attn_pipeline(q_ref, k_ref, v_ref, kc_ref, vc_ref, bias_ref, o_ref,
                   s0, s1, sc0, sc1, p0, p1, pc0, pc1, den0, den1, *, n_blocks, last_kb):
    s_bufs, sc_bufs, p_bufs, pc_bufs, den_bufs = (s0, s1), (sc0, sc1), (p0, p1), (pc0, pc1), (den0, den1)
    groups_per_item = n_blocks // ATTN_GROUP
    n_groups = q_ref.shape[0] * groups_per_item
    g_rows = ATTN_GROUP * Q_BLK
    contract_last = (((1,), (1,)), ((), ()))

    def key_start(t):
        kb = jnp.clip(Q_ROWS * t - WIN_H // 2, 0, last_kb)
        return pl.multiple_of(kb * GRID_W, GRID_W)

    def locate(g):
        item = g // groups_per_item
        local = g - item * groups_per_item
        return item, local, pl.multiple_of(local * g_rows, g_rows)

    def scores(g, slot):
        item, local, row0 = locate(g)
        q = q_ref[item, pl.ds(row0, g_rows), :]
        sc_bufs[slot][...] = lax.dot_general(q, kc_ref[item], contract_last, preferred_element_type=F32)
        for i in range(ATTN_GROUP):
            t = local * ATTN_GROUP + i
            geom = jnp.where(t < 2, t, jnp.where(t >= n_blocks - 2, t - (n_blocks - 5), 2))
            k = k_ref[item, pl.ds(key_start(t), K_BLK), :]
            s = lax.dot_general(q[i * Q_BLK:(i + 1) * Q_BLK], k, contract_last, preferred_element_type=F32)
            s_bufs[slot][i * Q_BLK:(i + 1) * Q_BLK, :] = s + bias_ref[geom]

    def softmax(slot):
        s = s_bufs[slot][...]
        sc = sc_bufs[slot][...]
        m = jnp.maximum(jnp.max(s, axis=-1, keepdims=True), jnp.max(sc, axis=-1, keepdims=True))
        p = jnp.exp2(s - m)
        pc = jnp.exp2(sc - m)
        den_bufs[slot][...] = jnp.sum(p, axis=-1, keepdims=True) + jnp.sum(pc, axis=-1, keepdims=True)
        p_bufs[slot][...] = p.astype(BF16)
        pc_bufs[slot][...] = pc.astype(BF16)

    def weighted_sum(g, slot):
        item, local, row0 = locate(g)
        o_ctx = _dot(pc_bufs[slot][...], vc_ref[item])
        outs = []
        for i in range(ATTN_GROUP):
            v = v_ref[item, pl.ds(key_start(local * ATTN_GROUP + i), K_BLK), :]
            outs.append(_dot(p_bufs[slot][i * Q_BLK:(i + 1) * Q_BLK, :], v))
        o = (jnp.concatenate(outs, axis=0) + o_ctx) / den_bufs[slot][...]
        o_ref[item, pl.ds(row0, g_rows), :] = o.astype(o_ref.dtype)

    scores(0, 0)
    scores(1, 1)
    softmax(0)

    def steady(j, carry):
        for parity in range(2):
            tau = 2 * j + 2 + parity
            scores(tau, parity)
            softmax(1 - parity)
            weighted_sum(tau - 2, parity)
        return carry

    lax.fori_loop(0, (n_groups - 2) // 2, steady, 0)
    softmax(1)
    weighted_sum(n_groups - 2, 0)
    weighted_sum(n_groups - 1, 1)


def _neighbourhood_attention(qkv, kvc, strip, cast_srcs=()):
    bsz, n, _ = qkv.shape
    ctx_len = kvc.shape[1]
    rows = n // GRID_W
    n_blocks = rows // Q_ROWS
    assert bsz % ATTN_ITEMS == 0 and n_blocks % ATTN_GROUP == 0
    n_groups = ATTN_ITEMS * (n_blocks // ATTN_GROUP)
    assert n_groups % 2 == 0 and n_groups >= 4
    n_batch_steps = bsz // ATTN_ITEMS
    jobs = [_cast_job(src, layer, N_HEADS * n_batch_steps) for src, layer in cast_srcs]
    cast_in, cast_out, cast_shapes = _cast_io(jobs, lambda h, b: h * n_batch_steps + b)
    layout = _attn_geometries(rows)
    kern = functools.partial(_attn_kernel, n_blocks=n_blocks, last_kb=rows - KEY_ROWS, n_cast=len(jobs), layout=layout)
    seq_spec = lambda off: pl.BlockSpec((ATTN_ITEMS, n, HEAD_DIM), lambda h, b: (b, 0, off + h))
    ctx_spec = lambda off: pl.BlockSpec((ATTN_ITEMS, ctx_len, HEAD_DIM), lambda h, b: (b, 0, off + h))
    g_rows = ATTN_GROUP * Q_BLK
    slot_pair = lambda cols, dtype: [pltpu.VMEM((g_rows, cols), dtype)] * 2
    return pl.pallas_call(
        kern,
        grid=(N_HEADS, n_batch_steps),
        in_specs=[
            seq_spec(0), seq_spec(N_HEADS), seq_spec(2 * N_HEADS),
            ctx_spec(0), ctx_spec(N_HEADS),
            pl.BlockSpec((1,) + strip.shape[1:], lambda h, b: (h, 0, 0, 0)),
        ] + cast_in,
        out_specs=[pl.BlockSpec((ATTN_ITEMS, n, HEAD_DIM), lambda h, b: (b, 0, h))] + cast_out,
        out_shape=[jax.ShapeDtypeStruct((bsz, n, D_MODEL), BF16)] + cast_shapes,
        scratch_shapes=(slot_pair(K_BLK, F32) + slot_pair(ctx_len, F32) + slot_pair(K_BLK, BF16)
                        + slot_pair(ctx_len, BF16) + slot_pair(1, F32)
                        + [pltpu.VMEM((len(layout), Q_BLK, K_BLK), F32)]),
        compiler_params=_params("arbitrary", "arbitrary"),
        name="neighbourhood_attention",
    )(qkv, qkv, qkv, kvc, kvc, strip, *[job.src for job in jobs])


def _proj_res_ln_kernel(y_ref, w_ref, h_ref, g_ref, lg_ref, lb_ref, o_ref):
    sub = y_ref.shape[0] // PROJ_SUB_BLOCKS
    for r in range(PROJ_SUB_BLOCKS):
        rows = slice(r * sub, (r + 1) * sub)
        o_ref[rows, :] = ALPHA * h_ref[rows, :] + g_ref[0] * _dot(y_ref[rows, :], w_ref[...])
        _layer_norm_in_place(o_ref, rows, lg_ref[...], lb_ref[...])


def _proj_res_ln(y, w, h, gate, ln_g, ln_b, rows_per_batch, tm=1024):
    m, k = y.shape
    d = w.shape[1]
    tiles_per_batch = rows_per_batch // tm
    return pl.pallas_call(
        _proj_res_ln_kernel,
        grid=(m // tm,),
        in_specs=[
            pl.BlockSpec((tm, k), lambda i: (i, 0)),
            pl.BlockSpec((k, d), lambda i: (0, 0), pipeline_mode=pl.Buffered(1)),
            pl.BlockSpec((tm, d), lambda i: (i, 0)),
            pl.BlockSpec((1, 1, d), lambda i: (i // tiles_per_batch, 0, 0)),
            pl.BlockSpec((1, d), lambda i: (0, 0)),
            pl.BlockSpec((1, d), lambda i: (0, 0)),
        ],
        out_specs=pl.BlockSpec((tm, d), lambda i: (i, 0)),
        out_shape=jax.ShapeDtypeStruct((m, d), F32),
        compiler_params=_params("parallel"),
        name="proj_res_ln",
    )(y, w, h, gate, ln_g.reshape(1, d), ln_b.reshape(1, d))


def _ffn_kernel(h_ref, sh_ref, sc_ref, g_ref, w_in_hbm, w_out_hbm, lg_ref, lb_ref, o_ref,
                a_ref, wg_buf, wu_buf, wo_buf, sem, *, n_f, tf):
    i = pl.program_id(0)
    n_tiles = pl.num_programs(0)
    d_ff = n_f * tf

    def chunk_copies(f, slot):
        c0 = pl.multiple_of(f * tf, tf)
        return (
            pltpu.make_async_copy(w_in_hbm.at[:, pl.ds(c0, tf)], wg_buf.at[slot], sem.at[slot, 0]),
            pltpu.make_async_copy(w_in_hbm.at[:, pl.ds(d_ff + c0, tf)], wu_buf.at[slot], sem.at[slot, 1]),
            pltpu.make_async_copy(w_out_hbm.at[pl.ds(c0, tf), :], wo_buf.at[slot], sem.at[slot, 2]),
        )

    @pl.when(i == 0)
    def _():
        for copy in chunk_copies(0, 0):
            copy.start()

    h = h_ref[...]
    a_ref[...] = (h * (1.0 + sc_ref[0]) + sh_ref[0]).astype(BF16)
    o_ref[...] = ALPHA * h

    def chunk(f, carry):
        slot = (i * n_f + f) % 2
        for copy in chunk_copies(f, slot):
            copy.wait()

        @pl.when((f + 1 < n_f) | (i + 1 < n_tiles))
        def _():
            for copy in chunk_copies((f + 1) % n_f, 1 - slot):
                copy.start()

        a = a_ref[...]
        act = _silu(_dot(a, wg_buf[slot])) * _dot(a, wu_buf[slot])
        o_ref[...] += g_ref[0] * _dot(act.astype(BF16), wo_buf[slot])
        return carry

    lax.fori_loop(0, n_f, chunk, 0)
    _layer_norm_in_place(o_ref, slice(0, o_ref.shape[0]), lg_ref[...], lb_ref[...])


def _ffn(h, sh, sc, gate, w_in, w_out, ln_g, ln_b, rows_per_batch, tm=1024, tf=512):
    m, d = h.shape
    d_ff = w_out.shape[0]
    n_f = d_ff // tf
    tiles_per_batch = rows_per_batch // tm
    mod_spec = pl.BlockSpec((1, 1, d), lambda i: (i // tiles_per_batch, 0, 0))
    vec_spec = pl.BlockSpec((1, d), lambda i: (0, 0))
    hbm_spec = pl.BlockSpec(memory_space=pl.ANY)
    return pl.pallas_call(
        functools.partial(_ffn_kernel, n_f=n_f, tf=tf),
        grid=(m // tm,),
        in_specs=[
            pl.BlockSpec((tm, d), lambda i: (i, 0)),
            mod_spec, mod_spec, mod_spec,
            hbm_spec, hbm_spec,
            vec_spec, vec_spec,
        ],
        out_specs=pl.BlockSpec((tm, d), lambda i: (i, 0)),
        out_shape=jax.ShapeDtypeStruct((m, d), F32),
        scratch_shapes=[
            pltpu.VMEM((tm, d), BF16),
            pltpu.VMEM((2, d, tf), BF16), pltpu.VMEM((2, d, tf), BF16), pltpu.VMEM((2, tf, d), BF16),
            pltpu.SemaphoreType.DMA((2, 3)),
        ],
        compiler_params=_params("arbitrary"),
        name="swiglu_ffn",
    )(h, sh, sc, gate, w_in, w_out, ln_g.reshape(1, d), ln_b.reshape(1, d))


def _ffn_streamed_kernel(sh_ref, sc_ref, g_ref, lg_ref, lb_ref, h_hbm, w_in_hbm, w_out_hbm, o_hbm,
                         hbuf, abuf, acc0, acc1, wg_buf, wu_buf, wo_buf, w_sem, h_sem, o_sem,
                         *, n_tiles, n_f, tf, tm, tiles_per_batch):
    piece, part = FFN_PIECE_ROWS, tm - FFN_PIECE_ROWS * (n_f - 1)
    d_ff = n_f * tf
    accs = (acc0, acc1)
    lg, lb = lg_ref[...], lb_ref[...]

    def tile_row0(t):
        return pl.multiple_of(t * tm, tm)

    def h_copy(t, slot):
        return pltpu.make_async_copy(h_hbm.at[pl.ds(tile_row0(t), tm), :], hbuf.at[slot], h_sem.at[slot])

    def out_copy(acc, t, row0, n_rows, k):
        return pltpu.make_async_copy(acc.at[pl.ds(row0, n_rows), :],
                                     o_hbm.at[pl.ds(tile_row0(t) + row0, n_rows), :], o_sem.at[k])

    def weight_copies(f, slot):
        c0 = pl.multiple_of(f * tf, tf)
        return (
            pltpu.make_async_copy(w_in_hbm.at[:, pl.ds(c0, tf)], wg_buf.at[slot], w_sem.at[slot, 0]),
            pltpu.make_async_copy(w_in_hbm.at[:, pl.ds(d_ff + c0, tf)], wu_buf.at[slot], w_sem.at[slot, 1]),
            pltpu.make_async_copy(w_out_hbm.at[pl.ds(c0, tf), :], wo_buf.at[slot], w_sem.at[slot, 2]),
        )

    def modulate(t, h):
        b = t // tiles_per_batch
        return (h * (1.0 + sc_ref[b]) + sh_ref[b]).astype(BF16)

    def layer_norm_rows(acc, row0, n_rows):
        for s in range(0, n_rows, LN_STRIP_ROWS):
            rows = pl.ds(pl.multiple_of(row0 + s, LN_STRIP_ROWS), LN_STRIP_ROWS)
            acc[rows, :] = _layer_norm(acc[rows, :], lg, lb)

    def tile(i, parity):
        cur, prev = accs[parity], accs[1 - parity]
        nxt = jnp.minimum(i + 1, n_tiles - 1)
        last_rows = pl.ds(piece * (n_f - 1), part)

        @pl.when(i >= 2)
        def _():
            out_copy(cur, i - 2, piece * (n_f - 1), part, 0).wait()

        @pl.when(i >= 1)
        def _():
            cur[last_rows, :] = ALPHA * hbuf[parity, last_rows, :]

        h_copy(nxt, 1 - parity).wait()

        @pl.when(i + 2 <= n_tiles)
        def _():
            h_copy(jnp.minimum(i + 2, n_tiles - 1), parity).start()

        gate = g_ref[i // tiles_per_batch]

        def chunk(f, carry):
            slot = (i * n_f + f) % 2
            for copy in weight_copies(f, slot):
                copy.wait()
            has_prev_piece = (f > 0) & (i > 0)

            @pl.when(has_prev_piece)
            def _():
                out_copy(prev, i - 1, 0, part, 0).wait()
                out_copy(prev, i - 1, 0, piece - part, 1).wait()

            @pl.when((f + 1 < n_f) | (i + 1 < n_tiles))
            def _():
                for copy in weight_copies((f + 1) % n_f, 1 - slot):
                    copy.start()

            is_last = f == n_f - 1
            ln_a = pl.multiple_of(f * piece, 32)
            ln_b = pl.multiple_of(jnp.where(is_last, tm, f * piece + part), 32)
            layer_norm_rows(prev, ln_a, part)
            layer_norm_rows(prev, ln_b, piece - part)
            src = pl.multiple_of(jnp.maximum(f - 1, 0) * piece, 32)
            dst = pl.multiple_of(jnp.where(f > 0, (f - 1) * piece, tm), 32)
            prev[pl.ds(dst, piece), :] = ALPHA * hbuf[1 - parity, pl.ds(src, piece), :]
            m0 = pl.multiple_of(jnp.minimum(f * piece, tm - piece), 32)
            abuf[1 - parity, pl.ds(m0, piece), :] = modulate(nxt, hbuf[1 - parity, pl.ds(m0, piece), :])
            a = abuf[parity]
            act = _silu(_dot(a, wg_buf[slot])) * _dot(a, wu_buf[slot])
            cur[0:tm, :] += gate * _dot(act.astype(BF16), wo_buf[slot])

            @pl.when(i > 0)
            def _():
                out_copy(prev, i - 1, ln_a, part, 0).start()

            @pl.when((i > 0) & jnp.logical_not(is_last))
            def _():
                out_copy(prev, i - 1, ln_b, piece - part, 1).start()

            return carry

        lax.fori_loop(0, n_f, chunk, 0)

    acc1[...] = jnp.zeros_like(acc1)
    acc0[pl.ds(tm, piece), :] = jnp.zeros((piece, acc0.shape[1]), F32)
    h_copy(0, 0).start()
    for copy in weight_copies(0, 0):
        copy.start()
    h_copy(0, 0).wait()
    h_copy(jnp.minimum(1, n_tiles - 1), 1).start()
    abuf[0] = modulate(0, hbuf[0])
    acc0[0:tm, :] = ALPHA * hbuf[0]

    def tile_pair(j, carry):
        tile(2 * j, 0)
        tile(2 * j + 1, 1)
        return carry

    lax.fori_loop(0, n_tiles // 2, tile_pair, 0)

    last = accs[(n_tiles - 1) % 2]
    out_copy(accs[n_tiles % 2], n_tiles - 2, piece * (n_f - 1), part, 0).wait()
    _layer_norm_in_place(last, slice(0, tm), lg, lb)
    final = out_copy(last, n_tiles - 1, 0, tm, 0)
    final.start()
    final.wait()


def _ffn_streamed(h, sh, sc, gate, w_in, w_out, ln_g, ln_b, rows_per_batch, tm=1024, tf=512):
    m, d = h.shape
    d_ff = w_out.shape[0]
    n_f, n_tiles = d_ff // tf, m // tm
    part = tm - FFN_PIECE_ROWS * (n_f - 1)
    assert n_tiles % 2 == 0 and n_tiles >= 2 and rows_per_batch % tm == 0
    assert 0 < part <= FFN_PIECE_ROWS and part % 32 == 0 and FFN_PIECE_ROWS % 32 == 0
    whole = lambda arr: pl.BlockSpec(arr.shape, lambda i: (0,) * arr.ndim)
    hbm_spec = pl.BlockSpec(memory_space=pl.ANY)
    lg2, lb2 = ln_g.reshape(1, d), ln_b.reshape(1, d)
    kern = functools.partial(_ffn_streamed_kernel, n_tiles=n_tiles, n_f=n_f, tf=tf, tm=tm,
                             tiles_per_batch=rows_per_batch // tm)
    return pl.pallas_call(
        kern,
        grid=(1,),
        in_specs=[whole(sh), whole(sc), whole(gate), whole(lg2), whole(lb2), hbm_spec, hbm_spec, hbm_spec],
        out_specs=hbm_spec,
        out_shape=jax.ShapeDtypeStruct((m, d), F32),
        scratch_shapes=[
            pltpu.VMEM((2, tm, d), F32), pltpu.VMEM((2, tm, d), BF16),
            pltpu.VMEM((tm + FFN_PIECE_ROWS, d), F32), pltpu.VMEM((tm + FFN_PIECE_ROWS, d), F32),
            pltpu.VMEM((2, d, tf), BF16), pltpu.VMEM((2, d, tf), BF16), pltpu.VMEM((2, tf, d), BF16),
            pltpu.SemaphoreType.DMA((2, 3)), pltpu.SemaphoreType.DMA((2,)), pltpu.SemaphoreType.DMA((2,)),
        ],
        compiler_params=_params("arbitrary", vmem_limit_bytes=FFN_VMEM_LIMIT_BYTES),
        name="swiglu_ffn",
    )(sh, sc, gate, lg2, lb2, h, w_in, w_out)


def _gmlp_kernel(*refs, n_cast):
    h_ref, sh_ref, sc_ref, wu_ref, wv_ref, lg_ref, lb_ref, ws_ref, bs_ref = refs[:9]
    o_ref = refs[9 + n_cast]
    u_ref, vf_ref, v_ref = refs[10 + 2 * n_cast:]
    _run_casts(refs[9:9 + n_cast], refs[10 + n_cast:10 + 2 * n_cast])

    def gelu(z):
        return 0.5 * z * (1.0 + lax.erf(z * (2.0 ** -0.5)))

    d = u_ref.shape[1]
    col_chunks = [slice(c * GMLP_COL_CHUNK, (c + 1) * GMLP_COL_CHUNK) for c in range(d // GMLP_COL_CHUNK)]
    for r0 in range(0, h_ref.shape[0], GMLP_SUB_ROWS):
        sub = slice(r0, r0 + GMLP_SUB_ROWS)
        a = (h_ref[sub, :] * (1.0 + sc_ref[0]) + sh_ref[0]).astype(BF16)
        for cols in col_chunks:
            vf_ref[sub, cols] = gelu(_dot(a, wv_ref[:, cols]))
        for s0 in range(r0, r0 + GMLP_SUB_ROWS, BF16_SUBLANES):
            strip = slice(s0, s0 + BF16_SUBLANES)
            v_ref[strip, :] = _layer_norm(vf_ref[strip, :], lg_ref[...], lb_ref[...]).astype(BF16)
        for cols in col_chunks:
            u_ref[sub, cols] = gelu(_dot(a, wu_ref[:, cols]))
        for c0 in range(r0, r0 + GMLP_SUB_ROWS, CHUNK):
            rows = slice(c0, c0 + CHUNK)
            for g in range(N_GROUPS):
                cols = slice(g * GROUP_DIM, (g + 1) * GROUP_DIM)
                mixed = _dot(ws_ref[g], v_ref[rows, cols]) + bs_ref[:, cols]
                o_ref[rows, cols] = (u_ref[rows, cols] * mixed).astype(o_ref.dtype)


def _gmlp_gate(h, sh, sc, w_in, ln_g, ln_b, w_s, b_s_full, rows_per_batch, tm=512, cast_srcs=()):
    m, d = h.shape
    tiles_per_batch = rows_per_batch // tm
    mod_spec = pl.BlockSpec((1, 1, d), lambda i: (i // tiles_per_batch, 0, 0))
    vec_spec = pl.BlockSpec((1, d), lambda i: (0, 0))
    jobs = [_cast_job(src, layer, m // tm) for src, layer in cast_srcs]
    cast_in, cast_out, cast_shapes = _cast_io(jobs, lambda i: i)
    return pl.pallas_call(
        functools.partial(_gmlp_kernel, n_cast=len(jobs)),
        grid=(m // tm,),
        in_specs=[
            pl.BlockSpec((tm, d), lambda i: (i, 0)),
            mod_spec, mod_spec,
            pl.BlockSpec((d, d), lambda i: (0, 0)),
            pl.BlockSpec((d, d), lambda i: (0, 1)),
            vec_spec, vec_spec,
            pl.BlockSpec((N_GROUPS, CHUNK, CHUNK), lambda i: (0, 0, 0)),
            pl.BlockSpec((CHUNK, d), lambda i: (0, 0)),
        ] + cast_in,
        out_specs=[pl.BlockSpec((tm, d), lambda i: (i, 0))] + cast_out,
        out_shape=[jax.ShapeDtypeStruct((m, d), BF16)] + cast_shapes,
        scratch_shapes=[pltpu.VMEM((tm, d), F32), pltpu.VMEM((tm, d), F32), pltpu.VMEM((tm, d), BF16)],
        compiler_params=_params("arbitrary"),
        name="gmlp_gate",
    )(h, sh, sc, w_in, w_in, ln_g.reshape(1, d), ln_b.reshape(1, d), w_s, b_s_full, *[job.src for job in jobs])


def kernel(x, c, ctx, c_ctx, ada_w, ada_b, ln_g, ln_b, na_w_qkv, na_w_o, na_rpb, gm_w_in, gm_ln_g, gm_ln_b,
           gm_w_s, gm_b_s, gm_w_out, ffn_w_in, ffn_w_out):
    bsz, n, d = x.shape
    ctx_len = ctx.shape[1]
    rows = n // GRID_W
    assert d == D_MODEL and n % (GRID_W * Q_ROWS) == 0 and rows >= KEY_ROWS + 2 * Q_ROWS

    cond = jnp.concatenate([c, c_ctx[None, :], jnp.zeros((8 - bsz - 1, d), F32)], axis=0)
    ada, w_qkv = _ada_params(cond, ada_w, ada_b, cast_srcs=[(na_w_qkv, 0)])

    def mod_vectors(layer):
        parts = jnp.split(ada[layer], 6, axis=-1)
        latent = [p[:bsz].reshape(bsz, 1, d) for p in parts]
        context = [p[bsz:bsz + 1].reshape(1, 1, d) for p in parts]
        return latent, context

    h = x.reshape(bsz * n, d)

    (sh1, sc1, g1, sh2, sc2, g2), (csh1, csc1, _, _, _, _) = mod_vectors(0)
    q_scale = jnp.concatenate([jnp.full((1, d), HEAD_DIM ** -0.5 * LOG2_E, F32), jnp.ones((1, 2 * d), F32)], axis=1)
    qkv, = _mod_matmul(h, sh1, sc1, w_qkv, q_scale, n, tm=1024, tn=2048, name="qkv_proj")
    kvc, = _mod_matmul(ctx.reshape(bsz * ctx_len, d), csh1, csc1, w_qkv, jnp.ones((1, 2 * d), F32), ctx_len,
                       tm=bsz * ctx_len, tn=1024, col_start=d, name="ctx_kv_proj")
    strip = _attn_bias_strip(na_rpb[0])
    o, w_o, gm_w_in_bf, gm_w_out_bf, ffn_w_in0, ffn_w_out0 = _neighbourhood_attention(
        qkv.reshape(bsz, n, 3 * d), kvc.reshape(bsz, ctx_len, 2 * d), strip,
        cast_srcs=[(na_w_o, 0), (gm_w_in, 0), (gm_w_out, 0), (ffn_w_in, 0), (ffn_w_out, 0)])
    h = _proj_res_ln(o.reshape(bsz * n, d), w_o, h, g1, ln_g[0, 0], ln_b[0, 0], n)
    h = _ffn_streamed(h, sh2, sc2, g2, ffn_w_in0, ffn_w_out0, ln_g[0, 1], ln_b[0, 1], n)

    (sh1, sc1, g1, sh2, sc2, g2), _ = mod_vectors(1)
    b_s_full = jnp.repeat(jnp.transpose(gm_b_s[0]), GROUP_DIM, axis=1)
    uv, ffn_w_in1, ffn_w_out1 = _gmlp_gate(h, sh1, sc1, gm_w_in_bf, gm_ln_g[0], gm_ln_b[0], gm_w_s[0].astype(BF16),
                                            b_s_full, n, cast_srcs=[(ffn_w_in, 1), (ffn_w_out, 1)])
    h = _proj_res_ln(uv, gm_w_out_bf, h, g1, ln_g[1, 0], ln_b[1, 0], n)
    h = _ffn_streamed(h, sh2, sc2, g2, ffn_w_in1, ffn_w_out1, ln_g[1, 1], ln_b[1, 1], n)
    return h.reshape(bsz, n, d)
```

```python
import functools
from typing import NamedTuple

import numpy as np
import jax
import jax.numpy as jnp
from jax import lax
from jax.experimental import pallas as pl
from jax.experimental.pallas import tpu as pltpu

D_MODEL = 2048
GRID_W = 64
N_HEADS = 16
HEAD_DIM = D_MODEL // N_HEADS
WIN_H = 8
WIN_W = 16
CHUNK = 128
N_GROUPS = 16
GROUP_DIM = D_MODEL // N_GROUPS
DEPTH = 2
ALPHA = (2 * DEPTH) ** 0.25
LN_EPS = 1e-5

VMEM_LIMIT_BYTES = 60 * 1024 * 1024
FFN_VMEM_LIMIT_BYTES = 63 * 1024 * 1024
MASK_VALUE = -1e30
LOG2_E = 1.4426950408889634

Q_ROWS = 2
KEY_ROWS = Q_ROWS + WIN_H - 1
Q_BLK = Q_ROWS * GRID_W
K_BLK = KEY_ROWS * GRID_W
GMLP_COL_CHUNK = 512
GMLP_SUB_ROWS = 256
PROJ_SUB_BLOCKS = 4
ATTN_GROUP = 8
ATTN_ITEMS = 2
STRIP_PAD_TILES = 1

BF16 = jnp.bfloat16
F32 = jnp.float32
BF16_SUBLANES = 16
LN_STRIP_ROWS = 8
FFN_LN_PIECE_ROWS = 128
FFN_MOD_PIECE_ROWS = 96


def _params(*sem, vmem_limit_bytes=VMEM_LIMIT_BYTES):
    return pltpu.CompilerParams(dimension_semantics=sem, vmem_limit_bytes=vmem_limit_bytes)


def _layer_norm(x, g, b):
    mu = jnp.mean(x, axis=-1, keepdims=True)
    xc = x - mu
    var = jnp.mean(xc * xc, axis=-1, keepdims=True)
    return xc * lax.rsqrt(var + LN_EPS) * g + b


def _layer_norm_in_place(ref, rows, g, b):
    for r0 in range(rows.start, rows.stop, LN_STRIP_ROWS):
        strip = slice(r0, r0 + LN_STRIP_ROWS)
        ref[strip, :] = _layer_norm(ref[strip, :], g, b)


def _silu(x):
    return x * (1.0 / (1.0 + jnp.exp(-x)))


def _dot(a, b):
    return jnp.dot(a, b, preferred_element_type=F32)


class _CastJob(NamedTuple):
    src: jax.Array
    layer: int
    rows: int
    steps_per_block: int


def _cast_job(src, layer, n_steps):
    r = src.shape[1]
    for steps_per_block in (1, 2, 4, 8):
        rows, rem = divmod(r * steps_per_block, n_steps)
        if rem == 0 and rows % BF16_SUBLANES == 0:
            return _CastJob(src, layer, rows, steps_per_block)
    raise ValueError(f"cannot split {r} rows over {n_steps} steps")


def _cast_io(jobs, step_of):
    in_specs, out_specs, out_shapes = [], [], []
    for job in jobs:
        _, r, c = job.src.shape
        in_specs.append(pl.BlockSpec(
            (None, job.rows, c), lambda *g, job=job: (job.layer, step_of(*g) // job.steps_per_block, 0)))
        out_specs.append(pl.BlockSpec(
            (job.rows, c), lambda *g, job=job: (step_of(*g) // job.steps_per_block, 0)))
        out_shapes.append(jax.ShapeDtypeStruct((r, c), BF16))
    return in_specs, out_specs, out_shapes


def _run_casts(src_refs, dst_refs):
    for src, dst in zip(src_refs, dst_refs):
        dst[...] = src[...].astype(dst.dtype)


def _ada_kernel(*refs, n_cast):
    cond_ref, w_ref, b_ref = refs[:3]
    o_ref = refs[3 + n_cast]
    s = _silu(cond_ref[...]).astype(BF16)
    o_ref[0] = _dot(s, w_ref[0].astype(BF16)) + b_ref[0]
    _run_casts(refs[3:3 + n_cast], refs[4 + n_cast:4 + 2 * n_cast])


def _ada_params(cond, ada_w, ada_b, tn=768, cast_srcs=()):
    depth, d, n6 = ada_w.shape
    rows = cond.shape[0]
    grid = (depth, n6 // tn)
    jobs = [_cast_job(src, layer, grid[0] * grid[1]) for src, layer in cast_srcs]
    cast_in, cast_out, cast_shapes = _cast_io(jobs, lambda l, j: l * grid[1] + j)
    return pl.pallas_call(
        functools.partial(_ada_kernel, n_cast=len(jobs)),
        grid=grid,
        in_specs=[
            pl.BlockSpec((rows, d), lambda l, j: (0, 0)),
            pl.BlockSpec((1, d, tn), lambda l, j: (l, 0, j)),
            pl.BlockSpec((1, 1, tn), lambda l, j: (l, 0, j)),
        ] + cast_in,
        out_specs=[pl.BlockSpec((1, rows, tn), lambda l, j: (l, 0, j))] + cast_out,
        out_shape=[jax.ShapeDtypeStruct((depth, rows, n6), F32)] + cast_shapes,
        compiler_params=_params("arbitrary", "arbitrary"),
        name="ada_params",
    )(cond, ada_w, ada_b.reshape(depth, 1, n6), *[job.src for job in jobs])


def _mod_matmul_kernel(*refs, n_cast):
    x_ref, sh_ref, sc_ref, w_ref, cs_ref = refs[:5]
    cast_src = refs[5:5 + n_cast]
    o_ref = refs[5 + n_cast]
    cast_dst = refs[6 + n_cast:6 + 2 * n_cast]
    a_ref = refs[6 + 2 * n_cast]

    @pl.when(pl.program_id(1) == 0)
    def _():
        a_ref[...] = (x_ref[...] * (1.0 + sc_ref[0]) + sh_ref[0]).astype(BF16)

    o_ref[...] = (_dot(a_ref[...], w_ref[...]) * cs_ref[...]).astype(o_ref.dtype)
    _run_casts(cast_src, cast_dst)


def _mod_matmul(x, sh, sc, w, col_scale, rows_per_batch, tm, tn, col_start=0, cast_srcs=(), name="mod_matmul"):
    m, d = x.shape
    n = w.shape[1] - col_start
    col_blk = col_start // tn
    tiles_per_batch = rows_per_batch // tm
    grid = (m // tm, n // tn)
    jobs = [_cast_job(src, layer, grid[0] * grid[1]) for src, layer in cast_srcs]
    cast_in, cast_out, cast_shapes = _cast_io(jobs, lambda i, j: i * grid[1] + j)
    if sh.shape[0] == 1:
        mod_idx = lambda i, j: (0, 0, 0)
    else:
        mod_idx = lambda i, j: (i // tiles_per_batch, 0, 0)
    return pl.pallas_call(
        functools.partial(_mod_matmul_kernel, n_cast=len(jobs)),
        grid=grid,
        in_specs=[
            pl.BlockSpec((tm, d), lambda i, j: (i, 0)),
            pl.BlockSpec((1, 1, d), mod_idx),
            pl.BlockSpec((1, 1, d), mod_idx),
            pl.BlockSpec((d, tn), lambda i, j: (0, col_blk + j)),
            pl.BlockSpec((1, tn), lambda i, j: (0, j)),
        ] + cast_in,
        out_specs=[pl.BlockSpec((tm, tn), lambda i, j: (i, j))] + cast_out,
        out_shape=[jax.ShapeDtypeStruct((m, n), BF16)] + cast_shapes,
        scratch_shapes=[pltpu.VMEM((tm, d), BF16)],
        compiler_params=_params("parallel", "arbitrary"),
        name=name,
    )(x, sh, sc, w, col_scale, *[job.src for job in jobs])


def _attn_geometries(rows):
    n_blocks = rows // Q_ROWS
    last_kb = rows - KEY_ROWS
    geoms = [(0, 0), (Q_ROWS, 0), (2 * Q_ROWS, 0),
             ((n_blocks - 2) * Q_ROWS, last_kb), ((n_blocks - 1) * Q_ROWS, last_kb)]
    layout = []
    for rb, kb in geoms:
        per_row = []
        for dr in range(Q_ROWS):
            r = rb + dr
            r0 = min(max(r - WIN_H // 2, 0), rows - WIN_H)
            first_tile = kb - r + (WIN_H - 1) + STRIP_PAD_TILES
            assert 0 <= first_tile and first_tile + KEY_ROWS <= 2 * WIN_H - 1 + 2 * STRIP_PAD_TILES
            per_row.append((first_tile, max(r0 - kb, 0), min(r0 - kb + WIN_H, KEY_ROWS)))
        layout.append(per_row)
    return layout


def _attn_bias_strip(rpb):
    n_heads, _, n_col_rel = rpb.shape
    c = np.arange(GRID_W)[:, None]
    kc = np.arange(GRID_W)[None, :]
    c0 = np.clip(c - WIN_W // 2, 0, GRID_W - WIN_W)
    col_ok = (kc >= c0) & (kc < c0 + WIN_W)
    select = (kc - c + WIN_W - 1)[None] == np.arange(n_col_rel)[:, None, None]
    select = jnp.asarray((select & col_ok[None]).astype(np.float32))
    tiles = jnp.einsum("hab,bck->hack", rpb * LOG2_E, select, precision=lax.Precision.HIGHEST)
    tiles = jnp.where(col_ok[None, None], tiles, MASK_VALUE)
    masked = jnp.full((n_heads, STRIP_PAD_TILES, GRID_W, GRID_W), MASK_VALUE, rpb.dtype)
    return jnp.concatenate([masked, tiles, masked], axis=1)


def _attn_kernel(*refs, n_blocks, last_kb, n_cast, layout):
    q_ref, k_ref, v_ref, kc_ref, vc_ref, strip_ref = refs[:6]
    o_ref = refs[6 + n_cast]
    s0, s1, sc0, sc1, p0, p1, pc0, pc1, den0, den1, bias_ref = refs[7 + 2 * n_cast:]

    @pl.when(pl.program_id(1) == 0)
    def _():
        key_row = lax.broadcasted_iota(jnp.int32, (GRID_W, K_BLK), 1) // GRID_W
        for geom, per_row in enumerate(layout):
            for dr, (first_tile, lo, hi) in enumerate(per_row):
                window = jnp.concatenate([strip_ref[0, first_tile + i] for i in range(KEY_ROWS)], axis=-1)
                valid = (key_row >= lo) & (key_row < hi)
                bias_ref[geom, dr * GRID_W:(dr + 1) * GRID_W, :] = jnp.where(valid, window, MASK_VALUE)

    _attn_pipeline(q_ref, k_ref, v_ref, kc_ref, vc_ref, bias_ref, o_ref,
                   s0, s1, sc0, sc1, p0, p1, pc0, pc1, den0, den1, n_blocks=n_blocks, last_kb=last_kb)
    _run_casts(refs[6:6 + n_cast], refs[7 + n_cast:7 + 2 * n_cast])


def _attn_pipeline(q_ref, k_ref, v_ref, kc_ref, vc_ref, bias_ref, o_ref,
                   s0, s1, sc0, sc1, p0, p1, pc0, pc1, den0, den1, *, n_blocks, last_kb):
    s_bufs, sc_bufs, p_bufs, pc_bufs, den_bufs = (s0, s1), (sc0, sc1), (p0, p1), (pc0, pc1), (den0, den1)
    groups_per_item = n_blocks // ATTN_GROUP
    n_groups = q_ref.shape[0] * groups_per_item
    g_rows = ATTN_GROUP * Q_BLK
    contract_last = (((1,), (1,)), ((), ()))

    def key_start(t):
        kb = jnp.clip(Q_ROWS * t - WIN_H // 2, 0, last_kb)
        return pl.multiple_of(kb * GRID_W, GRID_W)

    def locate(g):
        item = g // groups_per_item
        local = g - item * groups_per_item
        return item, local, pl.multiple_of(local * g_rows, g_rows)

    def scores(g, slot):
        item, local, row0 = locate(g)
        q = q_ref[item, pl.ds(row0, g_rows), :]
        sc_bufs[slot][...] = lax.dot_general(q, kc_ref[item], contract_last, preferred_element_type=F32)
        for i in range(ATTN_GROUP):
            t = local * ATTN_GROUP + i
            geom = jnp.where(t < 2, t, jnp.where(t >= n_blocks - 2, t - (n_blocks - 5), 2))
            k = k_ref[item, pl.ds(key_start(t), K_BLK), :]
            s = lax.dot_general(q[i * Q_BLK:(i + 1) * Q_BLK], k, contract_last, preferred_element_type=F32)
            s_bufs[slot][i * Q_BLK:(i + 1) * Q_BLK, :] = s + bias_ref[geom]

    def softmax(slot):
        s = s_bufs[slot][...]
        sc = sc_bufs[slot][...]
        m = jnp.maximum(jnp.max(s, axis=-1, keepdims=True), jnp.max(sc, axis=-1, keepdims=True))
        p = jnp.exp2(s - m)
        pc = jnp.exp2(sc - m)
        den_bufs[slot][...] = jnp.sum(p, axis=-1, keepdims=True) + jnp.sum(pc, axis=-1, keepdims=True)
        p_bufs[slot][...] = p.astype(BF16)
        pc_bufs[slot][...] = pc.astype(BF16)

    def weighted_sum(g, slot):
        item, local, row0 = locate(g)
        o_ctx = _dot(pc_bufs[slot][...], vc_ref[item])
        outs = []
        for i in range(ATTN_GROUP):
            v = v_ref[item, pl.ds(key_start(local * ATTN_GROUP + i), K_BLK), :]
            outs.append(_dot(p_bufs[slot][i * Q_BLK:(i + 1) * Q_BLK, :], v))
        o = (jnp.concatenate(outs, axis=0) + o_ctx) / den_bufs[slot][...]
        o_ref[item, pl.ds(row0, g_rows), :] = o.astype(o_ref.dtype)

    scores(0, 0)
    scores(1, 1)
    softmax(0)

    def steady(j, carry):
        for parity in range(2):
            tau = 2 * j + 2 + parity
            scores(tau, parity)
            softmax(1 - parity)
            weighted_sum(tau - 2, parity)
        return carry

    lax.fori_loop(0, (n_groups - 2) // 2, steady, 0)
    softmax(1)
    weighted_sum(n_groups - 2, 0)
    weighted_sum(n_groups - 1, 1)


def _neighbourhood_attention(qkv, kvc, strip, cast_srcs=()):
    bsz, n, _ = qkv.shape
    ctx_len = kvc.shape[1]
    rows = n // GRID_W
    n_blocks = rows // Q_ROWS
    assert bsz % ATTN_ITEMS == 0 and n_blocks % ATTN_GROUP == 0
    n_groups = ATTN_ITEMS * (n_blocks // ATTN_GROUP)
    assert n_groups % 2 == 0 and n_groups >= 4
    n_batch_steps = bsz // ATTN_ITEMS
    jobs = [_cast_job(src, layer, N_HEADS * n_batch_steps) for src, layer in cast_srcs]
    cast_in, cast_out, cast_shapes = _cast_io(jobs, lambda h, b: h * n_batch_steps + b)
    layout = _attn_geometries(rows)
    kern = functools.partial(_attn_kernel, n_blocks=n_blocks, last_kb=rows - KEY_ROWS, n_cast=len(jobs), layout=layout)
    seq_spec = lambda off: pl.BlockSpec((ATTN_ITEMS, n, HEAD_DIM), lambda h, b: (b, 0, off + h))
    ctx_spec = lambda off: pl.BlockSpec((ATTN_ITEMS, ctx_len, HEAD_DIM), lambda h, b: (b, 0, off + h))
    g_rows = ATTN_GROUP * Q_BLK
    slot_pair = lambda cols, dtype: [pltpu.VMEM((g_rows, cols), dtype)] * 2
    return pl.pallas_call(
        kern,
        grid=(N_HEADS, n_batch_steps),
        in_specs=[
            seq_spec(0), seq_spec(N_HEADS), seq_spec(2 * N_HEADS),
            ctx_spec(0), ctx_spec(N_HEADS),
            pl.BlockSpec((1,) + strip.shape[1:], lambda h, b: (h, 0, 0, 0)),
        ] + cast_in,
        out_specs=[pl.BlockSpec((ATTN_ITEMS, n, HEAD_DIM), lambda h, b: (b, 0, h))] + cast_out,
        out_shape=[jax.ShapeDtypeStruct((bsz, n, D_MODEL), BF16)] + cast_shapes,
        scratch_shapes=(slot_pair(K_BLK, F32) + slot_pair(ctx_len, F32) + slot_pair(K_BLK, BF16)
                        + slot_pair(ctx_len, BF16) + slot_pair(1, F32)
                        + [pltpu.VMEM((len(layout), Q_BLK, K_BLK), F32)]),
        compiler_params=_params("arbitrary", "arbitrary"),
        name="neighbourhood_attention",
    )(qkv, qkv, qkv, kvc, kvc, strip, *[job.src for job in jobs])


def _proj_res_ln_kernel(y_ref, w_ref, h_ref, g_ref, lg_ref, lb_ref, o_ref):
    sub = y_ref.shape[0] // PROJ_SUB_BLOCKS
    for r in range(PROJ_SUB_BLOCKS):
        rows = slice(r * sub, (r + 1) * sub)
        o_ref[rows, :] = ALPHA * h_ref[rows, :] + g_ref[0] * _dot(y_ref[rows, :], w_ref[...])
        _layer_norm_in_place(o_ref, rows, lg_ref[...], lb_ref[...])


def _proj_res_ln(y, w, h, gate, ln_g, ln_b, rows_per_batch, tm=1024):
    m, k = y.shape
    d = w.shape[1]
    tiles_per_batch = rows_per_batch // tm
    return pl.pallas_call(
        _proj_res_ln_kernel,
        grid=(m // tm,),
        in_specs=[
            pl.BlockSpec((tm, k), lambda i: (i, 0)),
            pl.BlockSpec((k, d), lambda i: (0, 0), pipeline_mode=pl.Buffered(1)),
            pl.BlockSpec((tm, d), lambda i: (i, 0)),
            pl.BlockSpec((1, 1, d), lambda i: (i // tiles_per_batch, 0, 0)),
            pl.BlockSpec((1, d), lambda i: (0, 0)),
            pl.BlockSpec((1, d), lambda i: (0, 0)),
        ],
        out_specs=pl.BlockSpec((tm, d), lambda i: (i, 0)),
        out_shape=jax.ShapeDtypeStruct((m, d), F32),
        compiler_params=_params("parallel"),
        name="proj_res_ln",
    )(y, w, h, gate, ln_g.reshape(1, d), ln_b.reshape(1, d))


def _ffn_kernel(h_ref, sh_ref, sc_ref, g_ref, w_in_hbm, w_out_hbm, lg_ref, lb_ref, o_ref,
                a_ref, wg_buf, wu_buf, wo_buf, sem, *, n_f, tf):
    i = pl.program_id(0)
    n_tiles = pl.num_programs(0)
    d_ff = n_f * tf

    def chunk_copies(f, slot):
        c0 = pl.multiple_of(f * tf, tf)
        return (
            pltpu.make_async_copy(w_in_hbm.at[:, pl.ds(c0, tf)], wg_buf.at[slot], sem.at[slot, 0]),
            pltpu.make_async_copy(w_in_hbm.at[:, pl.ds(d_ff + c0, tf)], wu_buf.at[slot], sem.at[slot, 1]),
            pltpu.make_async_copy(w_out_hbm.at[pl.ds(c0, tf), :], wo_buf.at[slot], sem.at[slot, 2]),
        )

    @pl.when(i == 0)
    def _():
        for copy in chunk_copies(0, 0):
            copy.start()

    h = h_ref[...]
    a_ref[...] = (h * (1.0 + sc_ref[0]) + sh_ref[0]).astype(BF16)
    o_ref[...] = ALPHA * h

    def chunk(f, carry):
        slot = (i * n_f + f) % 2
        for copy in chunk_copies(f, slot):
            copy.wait()

        @pl.when((f + 1 < n_f) | (i + 1 < n_tiles))
        def _():
            for copy in chunk_copies((f + 1) % n_f, 1 - slot):
                copy.start()

        a = a_ref[...]
        act = _silu(_dot(a, wg_buf[slot])) * _dot(a, wu_buf[slot])
        o_ref[...] += g_ref[0] * _dot(act.astype(BF16), wo_buf[slot])
        return carry

    lax.fori_loop(0, n_f, chunk, 0)
    _layer_norm_in_place(o_ref, slice(0, o_ref.shape[0]), lg_ref[...], lb_ref[...])


def _ffn(h, sh, sc, gate, w_in, w_out, ln_g, ln_b, rows_per_batch, tm=1024, tf=512):
    m, d = h.shape
    d_ff = w_out.shape[0]
    n_f = d_ff // tf
    tiles_per_batch = rows_per_batch // tm
    mod_spec = pl.BlockSpec((1, 1, d), lambda i: (i // tiles_per_batch, 0, 0))
    vec_spec = pl.BlockSpec((1, d), lambda i: (0, 0))
    hbm_spec = pl.BlockSpec(memory_space=pl.ANY)
    return pl.pallas_call(
        functools.partial(_ffn_kernel, n_f=n_f, tf=tf),
        grid=(m // tm,),
        in_specs=[
            pl.BlockSpec((tm, d), lambda i: (i, 0)),
            mod_spec, mod_spec, mod_spec,
            hbm_spec, hbm_spec,
            vec_spec, vec_spec,
        ],
        out_specs=pl.BlockSpec((tm, d), lambda i: (i, 0)),
        out_shape=jax.ShapeDtypeStruct((m, d), F32),
        scratch_shapes=[
            pltpu.VMEM((tm, d), BF16),
            pltpu.VMEM((2, d, tf), BF16), pltpu.VMEM((2, d, tf), BF16), pltpu.VMEM((2, tf, d), BF16),
            pltpu.SemaphoreType.DMA((2, 3)),
        ],
        compiler_params=_params("arbitrary"),
        name="swiglu_ffn",
    )(h, sh, sc, gate, w_in, w_out, ln_g.reshape(1, d), ln_b.reshape(1, d))


def _ffn_streamed_kernel(sh_ref, sc_ref, g_ref, lg_ref, lb_ref, h_hbm, w_in_hbm, w_out_hbm, o_hbm,
                         hbuf, abuf, acc0, acc1, wg_buf, wu_buf, wo_buf, w_sem, h_sem, o_sem,
                         *, n_tiles, n_f, tf, tm, tiles_per_batch):
    ln_piece, mod_piece = FFN_LN_PIECE_ROWS, FFN_MOD_PIECE_ROWS
    n_ln = tm // ln_piece
    d_ff = n_f * tf
    accs = (acc0, acc1)
    lg, lb = lg_ref[...], lb_ref[...]

    def tile_row0(t):
        return pl.multiple_of(t * tm, tm)

    def h_copy(t, slot):
        return pltpu.make_async_copy(h_hbm.at[pl.ds(tile_row0(t), tm), :], hbuf.at[slot], h_sem.at[slot])

    def out_copy(acc, t, row0, n_rows):
        return pltpu.make_async_copy(acc.at[pl.ds(row0, n_rows), :],
                                     o_hbm.at[pl.ds(tile_row0(t) + row0, n_rows), :], o_sem.at[0])

    def weight_copies(f, slot):
        c0 = pl.multiple_of(f * tf, tf)
        return (
            pltpu.make_async_copy(w_in_hbm.at[:, pl.ds(c0, tf)], wg_buf.at[slot], w_sem.at[slot, 0]),
            pltpu.make_async_copy(w_in_hbm.at[:, pl.ds(d_ff + c0, tf)], wu_buf.at[slot], w_sem.at[slot, 1]),
            pltpu.make_async_copy(w_out_hbm.at[pl.ds(c0, tf), :], wo_buf.at[slot], w_sem.at[slot, 2]),
        )

    def modulate(t, h):
        b = t // tiles_per_batch
        return (h * (1.0 + sc_ref[b]) + sh_ref[b]).astype(BF16)

    def layer_norm_rows(acc, row0, n_rows):
        for s in range(0, n_rows, LN_STRIP_ROWS):
            rows = pl.ds(pl.multiple_of(row0 + s, LN_STRIP_ROWS), LN_STRIP_ROWS)
            acc[rows, :] = _layer_norm(acc[rows, :], lg, lb)

    def tile(i, parity):
        cur, prev = accs[parity], accs[1 - parity]
        nxt = jnp.minimum(i + 1, n_tiles - 1)

        h_copy(nxt, 1 - parity).wait()

        @pl.when(i + 2 <= n_tiles)
        def _():
            h_copy(jnp.minimum(i + 2, n_tiles - 1), parity).start()

        gate = g_ref[i // tiles_per_batch]

        def chunk(f, carry):
            slot = (i * n_f + f) % 2
            for copy in weight_copies(f, slot):
                copy.wait()

            @pl.when((f + 1 < n_f) | (i + 1 < n_tiles))
            def _():
                for copy in weight_copies((f + 1) % n_f, 1 - slot):
                    copy.start()

            ln_row = pl.multiple_of(jnp.where(f < n_ln, f * ln_piece, tm), ln_piece)
            layer_norm_rows(prev, ln_row, ln_piece)
            q = f - 2
            seed_ok = (q >= 0) & (q < n_ln)
            src = pl.multiple_of(jnp.clip(q, 0, n_ln - 1) * ln_piece, ln_piece)
            dst = pl.multiple_of(jnp.where(seed_ok, q * ln_piece, tm), ln_piece)
            prev[pl.ds(dst, ln_piece), :] = ALPHA * hbuf[1 - parity, pl.ds(src, ln_piece), :]
            m0 = pl.multiple_of(jnp.minimum(f * mod_piece, tm - mod_piece), 32)
            abuf[1 - parity, pl.ds(m0, mod_piece), :] = modulate(nxt, hbuf[1 - parity, pl.ds(m0, mod_piece), :])
            a = abuf[parity]
            act = _silu(_dot(a, wg_buf[slot])) * _dot(a, wu_buf[slot])
            cur[0:tm, :] += gate * _dot(act.astype(BF16), wo_buf[slot])

            @pl.when((i > 0) & (f >= 1) & (f <= n_ln))
            def _():
                out_copy(prev, i - 1, 0, ln_piece).wait()

            @pl.when((i > 0) & (f < n_ln))
            def _():
                out_copy(prev, i - 1, ln_row, ln_piece).start()

            return carry

        lax.fori_loop(0, n_f, chunk, 0)

    acc1[...] = jnp.zeros_like(acc1)
    acc0[pl.ds(tm, ln_piece), :] = jnp.zeros((ln_piece, acc0.shape[1]), F32)
    h_copy(0, 0).start()
    for copy in weight_copies(0, 0):
        copy.start()
    h_copy(0, 0).wait()
    h_copy(jnp.minimum(1, n_tiles - 1), 1).start()
    abuf[0] = modulate(0, hbuf[0])
    acc0[0:tm, :] = ALPHA * hbuf[0]

    def tile_pair(j, carry):
        tile(2 * j, 0)
        tile(2 * j + 1, 1)
        return carry

    lax.fori_loop(0, n_tiles // 2, tile_pair, 0)

    last = accs[(n_tiles - 1) % 2]
    _layer_norm_in_place(last, slice(0, tm), lg, lb)
    final = out_copy(last, n_tiles - 1, 0, tm)
    final.start()
    final.wait()


def _ffn_streamed(h, sh, sc, gate, w_in, w_out, ln_g, ln_b, rows_per_batch, tm=1024, tf=512):
    m, d = h.shape
    d_ff = w_out.shape[0]
    n_f, n_tiles = d_ff // tf, m // tm
    assert n_tiles % 2 == 0 and n_tiles >= 2 and rows_per_batch % tm == 0
    assert tm % FFN_LN_PIECE_ROWS == 0 and tm // FFN_LN_PIECE_ROWS + 2 <= n_f
    assert FFN_MOD_PIECE_ROWS % 32 == 0 and FFN_MOD_PIECE_ROWS * n_f >= tm >= FFN_MOD_PIECE_ROWS
    whole = lambda arr: pl.BlockSpec(arr.shape, lambda i: (0,) * arr.ndim)
    hbm_spec = pl.BlockSpec(memory_space=pl.ANY)
    lg2, lb2 = ln_g.reshape(1, d), ln_b.reshape(1, d)
    kern = functools.partial(_ffn_streamed_kernel, n_tiles=n_tiles, n_f=n_f, tf=tf, tm=tm,
                             tiles_per_batch=rows_per_batch // tm)
    return pl.pallas_call(
        kern,
        grid=(1,),
        in_specs=[whole(sh), whole(sc), whole(gate), whole(lg2), whole(lb2), hbm_spec, hbm_spec, hbm_spec],
        out_specs=hbm_spec,
        out_shape=jax.ShapeDtypeStruct((m, d), F32),
        scratch_shapes=[
            pltpu.VMEM((2, tm, d), F32), pltpu.VMEM((2, tm, d), BF16),
            pltpu.VMEM((tm + FFN_LN_PIECE_ROWS, d), F32), pltpu.VMEM((tm + FFN_LN_PIECE_ROWS, d), F32),
            pltpu.VMEM((2, d, tf), BF16), pltpu.VMEM((2, d, tf), BF16), pltpu.VMEM((2, tf, d), BF16),
            pltpu.SemaphoreType.DMA((2, 3)), pltpu.SemaphoreType.DMA((2,)), pltpu.SemaphoreType.DMA((1,)),
        ],
        compiler_params=_params("arbitrary", vmem_limit_bytes=FFN_VMEM_LIMIT_BYTES),
        name="swiglu_ffn",
    )(sh, sc, gate, lg2, lb2, h, w_in, w_out)


def _gmlp_kernel(*refs, n_cast):
    h_ref, sh_ref, sc_ref, wu_ref, wv_ref, lg_ref, lb_ref, ws_ref, bs_ref = refs[:9]
    o_ref = refs[9 + n_cast]
    u_ref, vf_ref, v_ref = refs[10 + 2 * n_cast:]
    _run_casts(refs[9:9 + n_cast], refs[10 + n_cast:10 + 2 * n_cast])

    def gelu(z):
        return 0.5 * z * (1.0 + lax.erf(z * (2.0 ** -0.5)))

    d = u_ref.shape[1]
    col_chunks = [slice(c * GMLP_COL_CHUNK, (c + 1) * GMLP_COL_CHUNK) for c in range(d // GMLP_COL_CHUNK)]
    for r0 in range(0, h_ref.shape[0], GMLP_SUB_ROWS):
        sub = slice(r0, r0 + GMLP_SUB_ROWS)
        a = (h_ref[sub, :] * (1.0 + sc_ref[0]) + sh_ref[0]).astype(BF16)
        for cols in col_chunks:
            vf_ref[sub, cols] = gelu(_dot(a, wv_ref[:, cols]))
        for s0 in range(r0, r0 + GMLP_SUB_ROWS, BF16_SUBLANES):
            strip = slice(s0, s0 + BF16_SUBLANES)
            v_ref[strip, :] = _layer_norm(vf_ref[strip, :], lg_ref[...], lb_ref[...]).astype(BF16)
        for cols in col_chunks:
            u_ref[sub, cols] = gelu(_dot(a, wu_ref[:, cols]))
        for c0 in range(r0, r0 + GMLP_SUB_ROWS, CHUNK):
            rows = slice(c0, c0 + CHUNK)
            for g in range(N_GROUPS):
                cols = slice(g * GROUP_DIM, (g + 1) * GROUP_DIM)
                mixed = _dot(ws_ref[g], v_ref[rows, cols]) + bs_ref[:, cols]
                o_ref[rows, cols] = (u_ref[rows, cols] * mixed).astype(o_ref.dtype)


def _gmlp_gate(h, sh, sc, w_in, ln_g, ln_b, w_s, b_s_full, rows_per_batch, tm=512, cast_srcs=()):
    m, d = h.shape
    tiles_per_batch = rows_per_batch // tm
    mod_spec = pl.BlockSpec((1, 1, d), lambda i: (i // tiles_per_batch, 0, 0))
    vec_spec = pl.BlockSpec((1, d), lambda i: (0, 0))
    jobs = [_cast_job(src, layer, m // tm) for src, layer in cast_srcs]
    cast_in, cast_out, cast_shapes = _cast_io(jobs, lambda i: i)
    return pl.pallas_call(
        functools.partial(_gmlp_kernel, n_cast=len(jobs)),
        grid=(m // tm,),
        in_specs=[
            pl.BlockSpec((tm, d), lambda i: (i, 0)),
            mod_spec, mod_spec,
            pl.BlockSpec((d, d), lambda i: (0, 0)),
            pl.BlockSpec((d, d), lambda i: (0, 1)),
            vec_spec, vec_spec,
            pl.BlockSpec((N_GROUPS, CHUNK, CHUNK), lambda i: (0, 0, 0)),
            pl.BlockSpec((CHUNK, d), lambda i: (0, 0)),
        ] + cast_in,
        out_specs=[pl.BlockSpec((tm, d), lambda i: (i, 0))] + cast_out,
        out_shape=[jax.ShapeDtypeStruct((m, d), BF16)] + cast_shapes,
        scratch_shapes=[pltpu.VMEM((tm, d), F32), pltpu.VMEM((tm, d), F32), pltpu.VMEM((tm, d), BF16)],
        compiler_params=_params("arbitrary"),
        name="gmlp_gate",
    )(h, sh, sc, w_in, w_in, ln_g.reshape(1, d), ln_b.reshape(1, d), w_s, b_s_full, *[job.src for job in jobs])


def kernel(x, c, ctx, c_ctx, ada_w, ada_b, ln_g, ln_b, na_w_qkv, na_w_o, na_rpb, gm_w_in, gm_ln_g, gm_ln_b,
           gm_w_s, gm_b_s, gm_w_out, ffn_w_in, ffn_w_out):
    bsz, n, d = x.shape
    ctx_len = ctx.shape[1]
    rows = n // GRID_W
    assert d == D_MODEL and n % (GRID_W * Q_ROWS) == 0 and rows >= KEY_ROWS + 2 * Q_ROWS

    cond = jnp.concatenate([c, c_ctx[None, :], jnp.zeros((8 - bsz - 1, d), F32)], axis=0)
    ada, w_qkv = _ada_params(cond, ada_w, ada_b, cast_srcs=[(na_w_qkv, 0)])

    def mod_vectors(layer):
        parts = jnp.split(ada[layer], 6, axis=-1)
        latent = [p[:bsz].reshape(bsz, 1, d) for p in parts]
        context = [p[bsz:bsz + 1].reshape(1, 1, d) for p in parts]
        return latent, context

    h = x.reshape(bsz * n, d)

    (sh1, sc1, g1, sh2, sc2, g2), (csh1, csc1, _, _, _, _) = mod_vectors(0)
    q_scale = jnp.concatenate([jnp.full((1, d), HEAD_DIM ** -0.5 * LOG2_E, F32), jnp.ones((1, 2 * d), F32)], axis=1)
    qkv, = _mod_matmul(h, sh1, sc1, w_qkv, q_scale, n, tm=1024, tn=2048, name="qkv_proj")
    kvc, = _mod_matmul(ctx.reshape(bsz * ctx_len, d), csh1, csc1, w_qkv, jnp.ones((1, 2 * d), F32), ctx_len,
                       tm=bsz * ctx_len, tn=1024, col_start=d, name="ctx_kv_proj")
    strip = _attn_bias_strip(na_rpb[0])
    o, w_o, gm_w_in_bf, gm_w_out_bf, ffn_w_in0, ffn_w_out0 = _neighbourhood_attention(
        qkv.reshape(bsz, n, 3 * d), kvc.reshape(bsz, ctx_len, 2 * d), strip,
        cast_srcs=[(na_w_o, 0), (gm_w_in, 0), (gm_w_out, 0), (ffn_w_in, 0), (ffn_w_out, 0)])
    h = _proj_res_ln(o.reshape(bsz * n, d), w_o, h, g1, ln_g[0, 0], ln_b[0, 0], n)
    h = _ffn_streamed(h, sh2, sc2, g2, ffn_w_in0, ffn_w_out0, ln_g[0, 1], ln_b[0, 1], n)

    (sh1, sc1, g1, sh2, sc2, g2), _ = mod_vectors(1)
    b_s_full = jnp.repeat(jnp.transpose(gm_b_s[0]), GROUP_DIM, axis=1)
    uv, ffn_w_in1, ffn_w_out1 = _gmlp_gate(h, sh1, sc1, gm_w_in_bf, gm_ln_g[0], gm_ln_b[0], gm_w_s[0].astype(BF16),
                                            b_s_full, n, cast_srcs=[(ffn_w_in, 1), (ffn_w_out, 1)])
    h = _proj_res_ln(uv, gm_w_out_bf, h, g1, ln_g[1, 0], ln_b[1, 0], n)
    h = _ffn_streamed(h, sh2, sc2, g2, ffn_w_in1, ffn_w_out1, ln_g[1, 1], ln_b[1, 1], n)
    return h.reshape(bsz, n, d)
```

```python
import functools
from typing import NamedTuple

import numpy as np
import jax
import jax.numpy as jnp
from jax import lax
from jax.experimental import pallas as pl
from jax.experimental.pallas import tpu as pltpu

D_MODEL = 2048
GRID_W = 64
N_HEADS = 16
HEAD_DIM = D_MODEL // N_HEADS
WIN_H = 8
WIN_W = 16
CHUNK = 128
N_GROUPS = 16
GROUP_DIM = D_MODEL // N_GROUPS
DEPTH = 2
ALPHA = (2 * DEPTH) ** 0.25
LN_EPS = 1e-5

VMEM_LIMIT_BYTES = 60 * 1024 * 1024
MASK_VALUE = -1e30
LOG2_E = 1.4426950408889634

Q_ROWS = 2
KEY_ROWS = Q_ROWS + WIN_H - 1
Q_BLK = Q_ROWS * GRID_W
K_BLK = KEY_ROWS * GRID_W
GMLP_COL_CHUNK = 512
GMLP_SUB_ROWS = 256
PROJ_SUB_BLOCKS = 4
ATTN_GROUP = 8
ATTN_ITEMS = 2
STRIP_PAD_TILES = 1

BF16 = jnp.bfloat16
F32 = jnp.float32
BF16_SUBLANES = 16
LN_STRIP_ROWS = 8


def _params(*sem):
    return pltpu.CompilerParams(dimension_semantics=sem, vmem_limit_bytes=VMEM_LIMIT_BYTES)


def _layer_norm(x, g, b):
    mu = jnp.mean(x, axis=-1, keepdims=True)
    xc = x - mu
    var = jnp.mean(xc * xc, axis=-1, keepdims=True)
    return xc * lax.rsqrt(var + LN_EPS) * g + b


def _layer_norm_in_place(ref, rows, g, b):
    for r0 in range(rows.start, rows.stop, LN_STRIP_ROWS):
        strip = slice(r0, r0 + LN_STRIP_ROWS)
        ref[strip, :] = _layer_norm(ref[strip, :], g, b)


def _silu(x):
    return x * (1.0 / (1.0 + jnp.exp(-x)))


def _dot(a, b):
    return jnp.dot(a, b, preferred_element_type=F32)


class _CastJob(NamedTuple):
    src: jax.Array
    layer: int
    rows: int
    steps_per_block: int


def _cast_job(src, layer, n_steps):
    r = src.shape[1]
    for steps_per_block in (1, 2, 4, 8):
        rows, rem = divmod(r * steps_per_block, n_steps)
        if rem == 0 and rows % BF16_SUBLANES == 0:
            return _CastJob(src, layer, rows, steps_per_block)
    raise ValueError(f"cannot split {r} rows over {n_steps} steps")


def _cast_io(jobs, step_of):
    in_specs, out_specs, out_shapes = [], [], []
    for job in jobs:
        _, r, c = job.src.shape
        in_specs.append(pl.BlockSpec(
            (None, job.rows, c), lambda *g, job=job: (job.layer, step_of(*g) // job.steps_per_block, 0)))
        out_specs.append(pl.BlockSpec(
            (job.rows, c), lambda *g, job=job: (step_of(*g) // job.steps_per_block, 0)))
        out_shapes.append(jax.ShapeDtypeStruct((r, c), BF16))
    return in_specs, out_specs, out_shapes


def _run_casts(src_refs, dst_refs):
    for src, dst in zip(src_refs, dst_refs):
        dst[...] = src[...].astype(dst.dtype)


def _ada_kernel(*refs, n_cast):
    cond_ref, w_ref, b_ref = refs[:3]
    o_ref = refs[3 + n_cast]
    s = _silu(cond_ref[...]).astype(BF16)
    o_ref[0] = _dot(s, w_ref[0].astype(BF16)) + b_ref[0]
    _run_casts(refs[3:3 + n_cast], refs[4 + n_cast:4 + 2 * n_cast])


def _ada_params(cond, ada_w, ada_b, tn=768, cast_srcs=()):
    depth, d, n6 = ada_w.shape
    rows = cond.shape[0]
    grid = (depth, n6 // tn)
    jobs = [_cast_job(src, layer, grid[0] * grid[1]) for src, layer in cast_srcs]
    cast_in, cast_out, cast_shapes = _cast_io(jobs, lambda l, j: l * grid[1] + j)
    return pl.pallas_call(
        functools.partial(_ada_kernel, n_cast=len(jobs)),
        grid=grid,
        in_specs=[
            pl.BlockSpec((rows, d), lambda l, j: (0, 0)),
            pl.BlockSpec((1, d, tn), lambda l, j: (l, 0, j)),
            pl.BlockSpec((1, 1, tn), lambda l, j: (l, 0, j)),
        ] + cast_in,
        out_specs=[pl.BlockSpec((1, rows, tn), lambda l, j: (l, 0, j))] + cast_out,
        out_shape=[jax.ShapeDtypeStruct((depth, rows, n6), F32)] + cast_shapes,
        compiler_params=_params("arbitrary", "arbitrary"),
        name="ada_params",
    )(cond, ada_w, ada_b.reshape(depth, 1, n6), *[job.src for job in jobs])


def _mod_matmul_kernel(*refs, n_cast):
    x_ref, sh_ref, sc_ref, w_ref, cs_ref = refs[:5]
    cast_src = refs[5:5 + n_cast]
    o_ref = refs[5 + n_cast]
    cast_dst = refs[6 + n_cast:6 + 2 * n_cast]
    a_ref = refs[6 + 2 * n_cast]

    @pl.when(pl.program_id(1) == 0)
    def _():
        a_ref[...] = (x_ref[...] * (1.0 + sc_ref[0]) + sh_ref[0]).astype(BF16)

    o_ref[...] = (_dot(a_ref[...], w_ref[...]) * cs_ref[...]).astype(o_ref.dtype)
    _run_casts(cast_src, cast_dst)


def _mod_matmul(x, sh, sc, w, col_scale, rows_per_batch, tm, tn, col_start=0, cast_srcs=(), name="mod_matmul"):
    m, d = x.shape
    n = w.shape[1] - col_start
    col_blk = col_start // tn
    tiles_per_batch = rows_per_batch // tm
    grid = (m // tm, n // tn)
    jobs = [_cast_job(src, layer, grid[0] * grid[1]) for src, layer in cast_srcs]
    cast_in, cast_out, cast_shapes = _cast_io(jobs, lambda i, j: i * grid[1] + j)
    if sh.shape[0] == 1:
        mod_idx = lambda i, j: (0, 0, 0)
    else:
        mod_idx = lambda i, j: (i // tiles_per_batch, 0, 0)
    return pl.pallas_call(
        functools.partial(_mod_matmul_kernel, n_cast=len(jobs)),
        grid=grid,
        in_specs=[
            pl.BlockSpec((tm, d), lambda i, j: (i, 0)),
            pl.BlockSpec((1, 1, d), mod_idx),
            pl.BlockSpec((1, 1, d), mod_idx),
            pl.BlockSpec((d, tn), lambda i, j: (0, col_blk + j)),
            pl.BlockSpec((1, tn), lambda i, j: (0, j)),
        ] + cast_in,
        out_specs=[pl.BlockSpec((tm, tn), lambda i, j: (i, j))] + cast_out,
        out_shape=[jax.ShapeDtypeStruct((m, n), BF16)] + cast_shapes,
        scratch_shapes=[pltpu.VMEM((tm, d), BF16)],
        compiler_params=_params("parallel", "arbitrary"),
        name=name,
    )(x, sh, sc, w, col_scale, *[job.src for job in jobs])


def _attn_geometries(rows):
    n_blocks = rows // Q_ROWS
    last_kb = rows - KEY_ROWS
    geoms = [(0, 0), (Q_ROWS, 0), (2 * Q_ROWS, 0),
             ((n_blocks - 2) * Q_ROWS, last_kb), ((n_blocks - 1) * Q_ROWS, last_kb)]
    layout = []
    for rb, kb in geoms:
        per_row = []
        for dr in range(Q_ROWS):
            r = rb + dr
            r0 = min(max(r - WIN_H // 2, 0), rows - WIN_H)
            first_tile = kb - r + (WIN_H - 1) + STRIP_PAD_TILES
            assert 0 <= first_tile and first_tile + KEY_ROWS <= 2 * WIN_H - 1 + 2 * STRIP_PAD_TILES
            per_row.append((first_tile, max(r0 - kb, 0), min(r0 - kb + WIN_H, KEY_ROWS)))
        layout.append(per_row)
    return layout


def _attn_bias_strip(rpb):
    n_heads, _, n_col_rel = rpb.shape
    c = np.arange(GRID_W)[:, None]
    kc = np.arange(GRID_W)[None, :]
    c0 = np.clip(c - WIN_W // 2, 0, GRID_W - WIN_W)
    col_ok = (kc >= c0) & (kc < c0 + WIN_W)
    select = (kc - c + WIN_W - 1)[None] == np.arange(n_col_rel)[:, None, None]
    select = jnp.asarray((select & col_ok[None]).astype(np.float32))
    tiles = jnp.einsum("hab,bck->hack", rpb * LOG2_E, select, precision=lax.Precision.HIGHEST)
    tiles = jnp.where(col_ok[None, None], tiles, MASK_VALUE)
    masked = jnp.full((n_heads, STRIP_PAD_TILES, GRID_W, GRID_W), MASK_VALUE, rpb.dtype)
    return jnp.concatenate([masked, tiles, masked], axis=1)


def _attn_kernel(*refs, n_blocks, last_kb, n_cast, layout):
    q_ref, k_ref, v_ref, kc_ref, vc_ref, strip_ref = refs[:6]
    o_ref = refs[6 + n_cast]
    s0, s1, sc0, sc1, p0, p1, pc0, pc1, den0, den1, bias_ref = refs[7 + 2 * n_cast:]

    @pl.when(pl.program_id(1) == 0)
    def _():
        key_row = lax.broadcasted_iota(jnp.int32, (GRID_W, K_BLK), 1) // GRID_W
        for geom, per_row in enumerate(layout):
            for dr, (first_tile, lo, hi) in enumerate(per_row):
                window = jnp.concatenate([strip_ref[0, first_tile + i] for i in range(KEY_ROWS)], axis=-1)
                valid = (key_row >= lo) & (key_row < hi)
                bias_ref[geom, dr * GRID_W:(dr + 1) * GRID_W, :] = jnp.where(valid, window, MASK_VALUE)

    _attn_pipeline(q_ref, k_ref, v_ref, kc_ref, vc_ref, bias_ref, o_ref,
                   s0, s1, sc0, sc1, p0, p1, pc0, pc1, den0, den1, n_blocks=n_blocks, last_kb=last_kb)
    _run_casts(refs[6:6 + n_cast], refs[7 + n_cast:7 + 2 * n_cast])


def _attn_pipeline(q_ref, k_ref, v_ref, kc_ref, vc_ref, bias_ref, o_ref,
                   s0, s1, sc0, sc1, p0, p1, pc0, pc1, den0, den1, *, n_blocks, last_kb):
    s_bufs, sc_bufs, p_bufs, pc_bufs, den_bufs = (s0, s1), (sc0, sc1), (p0, p1), (pc0, pc1), (den0, den1)
    groups_per_item = n_blocks // ATTN_GROUP
    n_groups = q_ref.shape[0] * groups_per_item
    g_rows = ATTN_GROUP * Q_BLK
    contract_last = (((1,), (1,)), ((), ()))

    def key_start(t):
        kb = jnp.clip(Q_ROWS * t - WIN_H // 2, 0, last_kb)
        return pl.multiple_of(kb * GRID_W, GRID_W)

    def locate(g):
        item = g // groups_per_item
        local = g - item * groups_per_item
        return item, local, pl.multiple_of(local * g_rows, g_rows)

    def scores(g, slot):
        item, local, row0 = locate(g)
        q = q_ref[item, pl.ds(row0, g_rows), :]
        sc_bufs[slot][...] = lax.dot_general(q, kc_ref[item], contract_last, preferred_element_type=F32)
        for i in range(ATTN_GROUP):
            t = local * ATTN_GROUP + i
            geom = jnp.where(t < 2, t, jnp.where(t >= n_blocks - 2, t - (n_blocks - 5), 2))
            k = k_ref[item, pl.ds(key_start(t), K_BLK), :]
            s = lax.dot_general(q[i * Q_BLK:(i + 1) * Q_BLK], k, contract_last, preferred_element_type=F32)
            s_bufs[slot][i * Q_BLK:(i + 1) * Q_BLK, :] = s + bias_ref[geom]

    def softmax(slot):
        s = s_bufs[slot][...]
        sc = sc_bufs[slot][...]
        m = jnp.maximum(jnp.max(s, axis=-1, keepdims=True), jnp.max(sc, axis=-1, keepdims=True))
        p = jnp.exp2(s - m)
        pc = jnp.exp2(sc - m)
        den_bufs[slot][...] = jnp.sum(p, axis=-1, keepdims=True) + jnp.sum(pc, axis=-1, keepdims=True)
        p_bufs[slot][...] = p.astype(BF16)
        pc_bufs[slot][...] = pc.astype(BF16)

    def weighted_sum(g, slot):
        item, local, row0 = locate(g)
        o_ctx = _dot(pc_bufs[slot][...], vc_ref[item])
        outs = []
        for i in range(ATTN_GROUP):
            v = v_ref[item, pl.ds(key_start(local * ATTN_GROUP + i), K_BLK), :]
            outs.append(_dot(p_bufs[slot][i * Q_BLK:(i + 1) * Q_BLK, :], v))
        o = (jnp.concatenate(outs, axis=0) + o_ctx) / den_bufs[slot][...]
        o_ref[item, pl.ds(row0, g_rows), :] = o.astype(o_ref.dtype)

    scores(0, 0)
    scores(1, 1)
    softmax(0)

    def steady(j, carry):
        for parity in range(2):
            tau = 2 * j + 2 + parity
            scores(tau, parity)
            softmax(1 - parity)
            weighted_sum(tau - 2, parity)
        return carry

    lax.fori_loop(0, (n_groups - 2) // 2, steady, 0)
    softmax(1)
    weighted_sum(n_groups - 2, 0)
    weighted_sum(n_groups - 1, 1)


def _neighbourhood_attention(qkv, kvc, strip, cast_srcs=()):
    bsz, n, _ = qkv.shape
    ctx_len = kvc.shape[1]
    rows = n // GRID_W
    n_blocks = rows // Q_ROWS
    assert bsz % ATTN_ITEMS == 0 and n_blocks % ATTN_GROUP == 0
    n_groups = ATTN_ITEMS * (n_blocks // ATTN_GROUP)
    assert n_groups % 2 == 0 and n_groups >= 4
    n_batch_steps = bsz // ATTN_ITEMS
    jobs = [_cast_job(src, layer, N_HEADS * n_batch_steps) for src, layer in cast_srcs]
    cast_in, cast_out, cast_shapes = _cast_io(jobs, lambda h, b: h * n_batch_steps + b)
    layout = _attn_geometries(rows)
    kern = functools.partial(_attn_kernel, n_blocks=n_blocks, last_kb=rows - KEY_ROWS, n_cast=len(jobs), layout=layout)
    seq_spec = lambda off: pl.BlockSpec((ATTN_ITEMS, n, HEAD_DIM), lambda h, b: (b, 0, off + h))
    ctx_spec = lambda off: pl.BlockSpec((ATTN_ITEMS, ctx_len, HEAD_DIM), lambda h, b: (b, 0, off + h))
    g_rows = ATTN_GROUP * Q_BLK
    slot_pair = lambda cols, dtype: [pltpu.VMEM((g_rows, cols), dtype)] * 2
    return pl.pallas_call(
        kern,
        grid=(N_HEADS, n_batch_steps),
        in_specs=[
            seq_spec(0), seq_spec(N_HEADS), seq_spec(2 * N_HEADS),
            ctx_spec(0), ctx_spec(N_HEADS),
            pl.BlockSpec((1,) + strip.shape[1:], lambda h, b: (h, 0, 0, 0)),
        ] + cast_in,
        out_specs=[pl.BlockSpec((ATTN_ITEMS, n, HEAD_DIM), lambda h, b: (b, 0, h))] + cast_out,
        out_shape=[jax.ShapeDtypeStruct((bsz, n, D_MODEL), BF16)] + cast_shapes,
        scratch_shapes=(slot_pair(K_BLK, F32) + slot_pair(ctx_len, F32) + slot_pair(K_BLK, BF16)
                        + slot_pair(ctx_len, BF16) + slot_pair(1, F32)
                        + [pltpu.VMEM((len(layout), Q_BLK, K_BLK), F32)]),
        compiler_params=_params("arbitrary", "arbitrary"),
        name="neighbourhood_attention",
    )(qkv, qkv, qkv, kvc, kvc, strip, *[job.src for job in jobs])


def _proj_res_ln_kernel(y_ref, w_ref, h_ref, g_ref, lg_ref, lb_ref, o_ref):
    sub = y_ref.shape[0] // PROJ_SUB_BLOCKS
    for r in range(PROJ_SUB_BLOCKS):
        rows = slice(r * sub, (r + 1) * sub)
        o_ref[rows, :] = ALPHA * h_ref[rows, :] + g_ref[0] * _dot(y_ref[rows, :], w_ref[...])
        _layer_norm_in_place(o_ref, rows, lg_ref[...], lb_ref[...])


def _proj_res_ln(y, w, h, gate, ln_g, ln_b, rows_per_batch, tm=1024):
    m, k = y.shape
    d = w.shape[1]
    tiles_per_batch = rows_per_batch // tm
    return pl.pallas_call(
        _proj_res_ln_kernel,
        grid=(m // tm,),
        in_specs=[
            pl.BlockSpec((tm, k), lambda i: (i, 0)),
            pl.BlockSpec((k, d), lambda i: (0, 0), pipeline_mode=pl.Buffered(1)),
            pl.BlockSpec((tm, d), lambda i: (i, 0)),
            pl.BlockSpec((1, 1, d), lambda i: (i // tiles_per_batch, 0, 0)),
            pl.BlockSpec((1, d), lambda i: (0, 0)),
            pl.BlockSpec((1, d), lambda i: (0, 0)),
        ],
        out_specs=pl.BlockSpec((tm, d), lambda i: (i, 0)),
        out_shape=jax.ShapeDtypeStruct((m, d), F32),
        compiler_params=_params("parallel"),
        name="proj_res_ln",
    )(y, w, h, gate, ln_g.reshape(1, d), ln_b.reshape(1, d))


def _ffn_kernel(h_ref, sh_ref, sc_ref, g_ref, w_in_hbm, w_out_hbm, lg_ref, lb_ref, o_ref,
                a_ref, wg_buf, wu_buf, wo_buf, sem, *, n_f, tf):
    i = pl.program_id(0)
    n_tiles = pl.num_programs(0)
    d_ff = n_f * tf

    def chunk_copies(f, slot):
        c0 = pl.multiple_of(f * tf, tf)
        return (
            pltpu.make_async_copy(w_in_hbm.at[:, pl.ds(c0, tf)], wg_buf.at[slot], sem.at[slot, 0]),
            pltpu.make_async_copy(w_in_hbm.at[:, pl.ds(d_ff + c0, tf)], wu_buf.at[slot], sem.at[slot, 1]),
            pltpu.make_async_copy(w_out_hbm.at[pl.ds(c0, tf), :], wo_buf.at[slot], sem.at[slot, 2]),
        )

    @pl.when(i == 0)
    def _():
        for copy in chunk_copies(0, 0):
            copy.start()

    h = h_ref[...]
    a_ref[...] = (h * (1.0 + sc_ref[0]) + sh_ref[0]).astype(BF16)
    o_ref[...] = ALPHA * h

    def chunk(f, carry):
        slot = (i * n_f + f) % 2
        for copy in chunk_copies(f, slot):
            copy.wait()

        @pl.when((f + 1 < n_f) | (i + 1 < n_tiles))
        def _():
            for copy in chunk_copies((f + 1) % n_f, 1 - slot):
                copy.start()

        a = a_ref[...]
        act = _silu(_dot(a, wg_buf[slot])) * _dot(a, wu_buf[slot])
        o_ref[...] += g_ref[0] * _dot(act.astype(BF16), wo_buf[slot])
        return carry

    lax.fori_loop(0, n_f, chunk, 0)
    _layer_norm_in_place(o_ref, slice(0, o_ref.shape[0]), lg_ref[...], lb_ref[...])


def _ffn(h, sh, sc, gate, w_in, w_out, ln_g, ln_b, rows_per_batch, tm=1024, tf=512):
    m, d = h.shape
    d_ff = w_out.shape[0]
    n_f = d_ff // tf
    tiles_per_batch = rows_per_batch // tm
    mod_spec = pl.BlockSpec((1, 1, d), lambda i: (i // tiles_per_batch, 0, 0))
    vec_spec = pl.BlockSpec((1, d), lambda i: (0, 0))
    hbm_spec = pl.BlockSpec(memory_space=pl.ANY)
    return pl.pallas_call(
        functools.partial(_ffn_kernel, n_f=n_f, tf=tf),
        grid=(m // tm,),
        in_specs=[
            pl.BlockSpec((tm, d), lambda i: (i, 0)),
            mod_spec, mod_spec, mod_spec,
            hbm_spec, hbm_spec,
            vec_spec, vec_spec,
        ],
        out_specs=pl.BlockSpec((tm, d), lambda i: (i, 0)),
        out_shape=jax.ShapeDtypeStruct((m, d), F32),
        scratch_shapes=[
            pltpu.VMEM((tm, d), BF16),
            pltpu.VMEM((2, d, tf), BF16), pltpu.VMEM((2, d, tf), BF16), pltpu.VMEM((2, tf, d), BF16),
            pltpu.SemaphoreType.DMA((2, 3)),
        ],
        compiler_params=_params("arbitrary"),
        name="swiglu_ffn",
    )(h, sh, sc, gate, w_in, w_out, ln_g.reshape(1, d), ln_b.reshape(1, d))


def _gmlp_kernel(*refs, n_cast):
    h_ref, sh_ref, sc_ref, wu_ref, wv_ref, lg_ref, lb_ref, ws_ref, bs_ref = refs[:9]
    o_ref = refs[9 + n_cast]
    u_ref, vf_ref, v_ref = refs[10 + 2 * n_cast:]
    _run_casts(refs[9:9 + n_cast], refs[10 + n_cast:10 + 2 * n_cast])

    def gelu(z):
        return 0.5 * z * (1.0 + lax.erf(z * (2.0 ** -0.5)))

    d = u_ref.shape[1]
    col_chunks = [slice(c * GMLP_COL_CHUNK, (c + 1) * GMLP_COL_CHUNK) for c in range(d // GMLP_COL_CHUNK)]
    for r0 in range(0, h_ref.shape[0], GMLP_SUB_ROWS):
        sub = slice(r0, r0 + GMLP_SUB_ROWS)
        a = (h_ref[sub, :] * (1.0 + sc_ref[0]) + sh_ref[0]).astype(BF16)
        for cols in col_chunks:
            vf_ref[sub, cols] = gelu(_dot(a, wv_ref[:, cols]))
        for s0 in range(r0, r0 + GMLP_SUB_ROWS, BF16_SUBLANES):
            strip = slice(s0, s0 + BF16_SUBLANES)
            v_ref[strip, :] = _layer_norm(vf_ref[strip, :], lg_ref[...], lb_ref[...]).astype(BF16)
        for cols in col_chunks:
            u_ref[sub, cols] = gelu(_dot(a, wu_ref[:, cols]))
        for c0 in range(r0, r0 + GMLP_SUB_ROWS, CHUNK):
            rows = slice(c0, c0 + CHUNK)
            for g in range(N_GROUPS):
                cols = slice(g * GROUP_DIM, (g + 1) * GROUP_DIM)
                mixed = _dot(ws_ref[g], v_ref[rows, cols]) + bs_ref[:, cols]
                o_ref[rows, cols] = (u_ref[rows, cols] * mixed).astype(o_ref.dtype)


def _gmlp_gate(h, sh, sc, w_in, ln_g, ln_b, w_s, b_s_full, rows_per_batch, tm=512, cast_srcs=()):
    m, d = h.shape
    tiles_per_batch = rows_per_batch // tm
    mod_spec = pl.BlockSpec((1, 1, d), lambda i: (i // tiles_per_batch, 0, 0))
    vec_spec = pl.BlockSpec((1, d), lambda i: (0, 0))
    jobs = [_cast_job(src, layer, m // tm) for src, layer in cast_srcs]
    cast_in, cast_out, cast_shapes = _cast_io(jobs, lambda i: i)
    return pl.pallas_call(
        functools.partial(_gmlp_kernel, n_cast=len(jobs)),
        grid=(m // tm,),
        in_specs=[
            pl.BlockSpec((tm, d), lambda i: (i, 0)),
            mod_spec, mod_spec,
            pl.BlockSpec((d, d), lambda i: (0, 0)),
            pl.BlockSpec((d, d), lambda i: (0, 1)),
            vec_spec, vec_spec,
            pl.BlockSpec((N_GROUPS, CHUNK, CHUNK), lambda i: (0, 0, 0)),
            pl.BlockSpec((CHUNK, d), lambda i: (0, 0)),
        ] + cast_in,
        out_specs=[pl.BlockSpec((tm, d), lambda i: (i, 0))] + cast_out,
        out_shape=[jax.ShapeDtypeStruct((m, d), BF16)] + cast_shapes,
        scratch_shapes=[pltpu.VMEM((tm, d), F32), pltpu.VMEM((tm, d), F32), pltpu.VMEM((tm, d), BF16)],
        compiler_params=_params("arbitrary"),
        name="gmlp_gate",
    )(h, sh, sc, w_in, w_in, ln_g.reshape(1, d), ln_b.reshape(1, d), w_s, b_s_full, *[job.src for job in jobs])


def kernel(x, c, ctx, c_ctx, ada_w, ada_b, ln_g, ln_b, na_w_qkv, na_w_o, na_rpb, gm_w_in, gm_ln_g, gm_ln_b,
           gm_w_s, gm_b_s, gm_w_out, ffn_w_in, ffn_w_out):
    bsz, n, d = x.shape
    ctx_len = ctx.shape[1]
    rows = n // GRID_W
    assert d == D_MODEL and n % (GRID_W * Q_ROWS) == 0 and rows >= KEY_ROWS + 2 * Q_ROWS

    cond = jnp.concatenate([c, c_ctx[None, :], jnp.zeros((8 - bsz - 1, d), F32)], axis=0)
    ada, w_qkv = _ada_params(cond, ada_w, ada_b, cast_srcs=[(na_w_qkv, 0)])

    def mod_vectors(layer):
        parts = jnp.split(ada[layer], 6, axis=-1)
        latent = [p[:bsz].reshape(bsz, 1, d) for p in parts]
        context = [p[bsz:bsz + 1].reshape(1, 1, d) for p in parts]
        return latent, context

    h = x.reshape(bsz * n, d)

    (sh1, sc1, g1, sh2, sc2, g2), (csh1, csc1, _, _, _, _) = mod_vectors(0)
    q_scale = jnp.concatenate([jnp.full((1, d), HEAD_DIM ** -0.5 * LOG2_E, F32), jnp.ones((1, 2 * d), F32)], axis=1)
    qkv, = _mod_matmul(h, sh1, sc1, w_qkv, q_scale, n, tm=512, tn=6144, name="qkv_proj")
    kvc, = _mod_matmul(ctx.reshape(bsz * ctx_len, d), csh1, csc1, w_qkv, jnp.ones((1, 2 * d), F32), ctx_len,
                       tm=bsz * ctx_len, tn=1024, col_start=d, name="ctx_kv_proj")
    strip = _attn_bias_strip(na_rpb[0])
    o, w_o, gm_w_in_bf, gm_w_out_bf, ffn_w_in0, ffn_w_out0 = _neighbourhood_attention(
        qkv.reshape(bsz, n, 3 * d), kvc.reshape(bsz, ctx_len, 2 * d), strip,
        cast_srcs=[(na_w_o, 0), (gm_w_in, 0), (gm_w_out, 0), (ffn_w_in, 0), (ffn_w_out, 0)])
    h = _proj_res_ln(o.reshape(bsz * n, d), w_o, h, g1, ln_g[0, 0], ln_b[0, 0], n)
    h = _ffn(h, sh2, sc2, g2, ffn_w_in0, ffn_w_out0, ln_g[0, 1], ln_b[0, 1], n)

    (sh1, sc1, g1, sh2, sc2, g2), _ = mod_vectors(1)
    b_s_full = jnp.repeat(jnp.transpose(gm_b_s[0]), GROUP_DIM, axis=1)
    uv, ffn_w_in1, ffn_w_out1 = _gmlp_gate(h, sh1, sc1, gm_w_in_bf, gm_ln_g[0], gm_ln_b[0], gm_w_s[0].astype(BF16),
                                            b_s_full, n, cast_srcs=[(ffn_w_in, 1), (ffn_w_out, 1)])
    h = _proj_res_ln(uv, gm_w_out_bf, h, g1, ln_g[1, 0], ln_b[1, 0], n)
    h = _ffn(h, sh2, sc2, g2, ffn_w_in1, ffn_w_out1, ln_g[1, 1], ln_b[1, 1], n)
    return h.reshape(bsz, n, d)
```

```python
import functools
from typing import NamedTuple

import numpy as np
import jax
import jax.numpy as jnp
from jax import lax
from jax.experimental import pallas as pl
from jax.experimental.pallas import tpu as pltpu

D_MODEL = 2048
GRID_W = 64
N_HEADS = 16
HEAD_DIM = D_MODEL // N_HEADS
WIN_H = 8
WIN_W = 16
CHUNK = 128
N_GROUPS = 16
GROUP_DIM = D_MODEL // N_GROUPS
DEPTH = 2
ALPHA = (2 * DEPTH) ** 0.25
LN_EPS = 1e-5

VMEM_LIMIT_BYTES = 60 * 1024 * 1024
MASK_VALUE = -1e30
LOG2_E = 1.4426950408889634

Q_ROWS = 2
KEY_ROWS = Q_ROWS + WIN_H - 1
Q_BLK = Q_ROWS * GRID_W
K_BLK = KEY_ROWS * GRID_W
GMLP_COL_CHUNK = 512
GMLP_SUB_ROWS = 256
PROJ_SUB_BLOCKS = 4
ATTN_GROUP = 8
ATTN_ITEMS = 2
STRIP_PAD_TILES = 1

BF16 = jnp.bfloat16
F32 = jnp.float32
BF16_SUBLANES = 16
LN_STRIP_ROWS = 8


def _params(*sem):
    return pltpu.CompilerParams(dimension_semantics=sem, vmem_limit_bytes=VMEM_LIMIT_BYTES)


def _layer_norm(x, g, b):
    mu = jnp.mean(x, axis=-1, keepdims=True)
    xc = x - mu
    var = jnp.mean(xc * xc, axis=-1, keepdims=True)
    return xc * lax.rsqrt(var + LN_EPS) * g + b


def _layer_norm_in_place(ref, rows, g, b):
    for r0 in range(rows.start, rows.stop, LN_STRIP_ROWS):
        strip = slice(r0, r0 + LN_STRIP_ROWS)
        ref[strip, :] = _layer_norm(ref[strip, :], g, b)


def _silu(x):
    return x * (1.0 / (1.0 + jnp.exp(-x)))


def _dot(a, b):
    return jnp.dot(a, b, preferred_element_type=F32)


class _CastJob(NamedTuple):
    src: jax.Array
    layer: int
    rows: int
    steps_per_block: int


def _cast_job(src, layer, n_steps):
    r = src.shape[1]
    for steps_per_block in (1, 2, 4, 8):
        rows, rem = divmod(r * steps_per_block, n_steps)
        if rem == 0 and rows % BF16_SUBLANES == 0:
            return _CastJob(src, layer, rows, steps_per_block)
    raise ValueError(f"cannot split {r} rows over {n_steps} steps")


def _cast_io(jobs, step_of):
    in_specs, out_specs, out_shapes = [], [], []
    for job in jobs:
        _, r, c = job.src.shape
        in_specs.append(pl.BlockSpec(
            (None, job.rows, c), lambda *g, job=job: (job.layer, step_of(*g) // job.steps_per_block, 0)))
        out_specs.append(pl.BlockSpec(
            (job.rows, c), lambda *g, job=job: (step_of(*g) // job.steps_per_block, 0)))
        out_shapes.append(jax.ShapeDtypeStruct((r, c), BF16))
    return in_specs, out_specs, out_shapes


def _run_casts(src_refs, dst_refs):
    for src, dst in zip(src_refs, dst_refs):
        dst[...] = src[...].astype(dst.dtype)


def _ada_kernel(*refs, n_cast):
    cond_ref, w_ref, b_ref = refs[:3]
    o_ref = refs[3 + n_cast]
    s = _silu(cond_ref[...]).astype(BF16)
    o_ref[0] = _dot(s, w_ref[0].astype(BF16)) + b_ref[0]
    _run_casts(refs[3:3 + n_cast], refs[4 + n_cast:4 + 2 * n_cast])


def _ada_params(cond, ada_w, ada_b, tn=768, cast_srcs=()):
    depth, d, n6 = ada_w.shape
    rows = cond.shape[0]
    grid = (depth, n6 // tn)
    jobs = [_cast_job(src, layer, grid[0] * grid[1]) for src, layer in cast_srcs]
    cast_in, cast_out, cast_shapes = _cast_io(jobs, lambda l, j: l * grid[1] + j)
    return pl.pallas_call(
        functools.partial(_ada_kernel, n_cast=len(jobs)),
        grid=grid,
        in_specs=[
            pl.BlockSpec((rows, d), lambda l, j: (0, 0)),
            pl.BlockSpec((1, d, tn), lambda l, j: (l, 0, j)),
            pl.BlockSpec((1, 1, tn), lambda l, j: (l, 0, j)),
        ] + cast_in,
        out_specs=[pl.BlockSpec((1, rows, tn), lambda l, j: (l, 0, j))] + cast_out,
        out_shape=[jax.ShapeDtypeStruct((depth, rows, n6), F32)] + cast_shapes,
        compiler_params=_params("arbitrary", "arbitrary"),
        name="ada_params",
    )(cond, ada_w, ada_b.reshape(depth, 1, n6), *[job.src for job in jobs])


def _mod_matmul_kernel(*refs, n_cast):
    x_ref, sh_ref, sc_ref, w_ref, cs_ref = refs[:5]
    cast_src = refs[5:5 + n_cast]
    o_ref = refs[5 + n_cast]
    cast_dst = refs[6 + n_cast:6 + 2 * n_cast]
    a_ref = refs[6 + 2 * n_cast]

    @pl.when(pl.program_id(1) == 0)
    def _():
        a_ref[...] = (x_ref[...] * (1.0 + sc_ref[0]) + sh_ref[0]).astype(BF16)

    o_ref[...] = (_dot(a_ref[...], w_ref[...]) * cs_ref[...]).astype(o_ref.dtype)
    _run_casts(cast_src, cast_dst)


def _mod_matmul(x, sh, sc, w, col_scale, rows_per_batch, tm, tn, col_start=0, cast_srcs=(), name="mod_matmul"):
    m, d = x.shape
    n = w.shape[1] - col_start
    col_blk = col_start // tn
    tiles_per_batch = rows_per_batch // tm
    grid = (m // tm, n // tn)
    jobs = [_cast_job(src, layer, grid[0] * grid[1]) for src, layer in cast_srcs]
    cast_in, cast_out, cast_shapes = _cast_io(jobs, lambda i, j: i * grid[1] + j)
    if sh.shape[0] == 1:
        mod_idx = lambda i, j: (0, 0, 0)
    else:
        mod_idx = lambda i, j: (i // tiles_per_batch, 0, 0)
    return pl.pallas_call(
        functools.partial(_mod_matmul_kernel, n_cast=len(jobs)),
        grid=grid,
        in_specs=[
            pl.BlockSpec((tm, d), lambda i, j: (i, 0)),
            pl.BlockSpec((1, 1, d), mod_idx),
            pl.BlockSpec((1, 1, d), mod_idx),
            pl.BlockSpec((d, tn), lambda i, j: (0, col_blk + j)),
            pl.BlockSpec((1, tn), lambda i, j: (0, j)),
        ] + cast_in,
        out_specs=[pl.BlockSpec((tm, tn), lambda i, j: (i, j))] + cast_out,
        out_shape=[jax.ShapeDtypeStruct((m, n), BF16)] + cast_shapes,
        scratch_shapes=[pltpu.VMEM((tm, d), BF16)],
        compiler_params=_params("parallel", "arbitrary"),
        name=name,
    )(x, sh, sc, w, col_scale, *[job.src for job in jobs])


def _attn_geometries(rows):
    n_blocks = rows // Q_ROWS
    last_kb = rows - KEY_ROWS
    geoms = [(0, 0), (Q_ROWS, 0), (2 * Q_ROWS, 0),
             ((n_blocks - 2) * Q_ROWS, last_kb), ((n_blocks - 1) * Q_ROWS, last_kb)]
    layout = []
    for rb, kb in geoms:
        per_row = []
        for dr in range(Q_ROWS):
            r = rb + dr
            r0 = min(max(r - WIN_H // 2, 0), rows - WIN_H)
            first_tile = kb - r + (WIN_H - 1) + STRIP_PAD_TILES
            assert 0 <= first_tile and first_tile + KEY_ROWS <= 2 * WIN_H - 1 + 2 * STRIP_PAD_TILES
            per_row.append((first_tile, max(r0 - kb, 0), min(r0 - kb + WIN_H, KEY_ROWS)))
        layout.append(per_row)
    return layout


def _attn_bias_strip(rpb):
    n_heads, _, n_col_rel = rpb.shape
    c = np.arange(GRID_W)[:, None]
    kc = np.arange(GRID_W)[None, :]
    c0 = np.clip(c - WIN_W // 2, 0, GRID_W - WIN_W)
    col_ok = (kc >= c0) & (kc < c0 + WIN_W)
    select = (kc - c + WIN_W - 1)[None] == np.arange(n_col_rel)[:, None, None]
    select = jnp.asarray((select & col_ok[None]).astype(np.float32))
    tiles = jnp.einsum("hab,bck->hack", rpb * LOG2_E, select, precision=lax.Precision.HIGHEST)
    tiles = jnp.where(col_ok[None, None], tiles, MASK_VALUE)
    masked = jnp.full((n_heads, STRIP_PAD_TILES, GRID_W, GRID_W), MASK_VALUE, rpb.dtype)
    return jnp.concatenate([masked, tiles, masked], axis=1)


def _attn_kernel(*refs, n_blocks, last_kb, n_cast, layout):
    q_ref, k_ref, v_ref, kc_ref, vc_ref, strip_ref = refs[:6]
    o_ref = refs[6 + n_cast]
    s0, s1, sc0, sc1, p0, p1, pc0, pc1, den0, den1, bias_ref = refs[7 + 2 * n_cast:]

    @pl.when(pl.program_id(1) == 0)
    def _():
        key_row = lax.broadcasted_iota(jnp.int32, (GRID_W, K_BLK), 1) // GRID_W
        for geom, per_row in enumerate(layout):
            for dr, (first_tile, lo, hi) in enumerate(per_row):
                window = jnp.concatenate([strip_ref[0, first_tile + i] for i in range(KEY_ROWS)], axis=-1)
                valid = (key_row >= lo) & (key_row < hi)
                bias_ref[geom, dr * GRID_W:(dr + 1) * GRID_W, :] = jnp.where(valid, window, MASK_VALUE)

    _attn_pipeline(q_ref, k_ref, v_ref, kc_ref, vc_ref, bias_ref, o_ref,
                   s0, s1, sc0, sc1, p0, p1, pc0, pc1, den0, den1, n_blocks=n_blocks, last_kb=last_kb)
    _run_casts(refs[6:6 + n_cast], refs[7 + n_cast:7 + 2 * n_cast])


def _attn_pipeline(q_ref, k_ref, v_ref, kc_ref, vc_ref, bias_ref, o_ref,
                   s0, s1, sc0, sc1, p0, p1, pc0, pc1, den0, den1, *, n_blocks, last_kb):
    s_bufs, sc_bufs, p_bufs, pc_bufs, den_bufs = (s0, s1), (sc0, sc1), (p0, p1), (pc0, pc1), (den0, den1)
    groups_per_item = n_blocks // ATTN_GROUP
    n_groups = q_ref.shape[0] * groups_per_item
    g_rows = ATTN_GROUP * Q_BLK
    contract_last = (((1,), (1,)), ((), ()))

    def key_start(t):
        kb = jnp.clip(Q_ROWS * t - WIN_H // 2, 0, last_kb)
        return pl.multiple_of(kb * GRID_W, GRID_W)

    def locate(g):
        item = g // groups_per_item
        local = g - item * groups_per_item
        return item, local, pl.multiple_of(local * g_rows, g_rows)

    def scores(g, slot):
        item, local, row0 = locate(g)
        q = q_ref[item, pl.ds(row0, g_rows), :]
        sc_bufs[slot][...] = lax.dot_general(q, kc_ref[item], contract_last, preferred_element_type=F32)
        for i in range(ATTN_GROUP):
            t = local * ATTN_GROUP + i
            geom = jnp.where(t < 2, t, jnp.where(t >= n_blocks - 2, t - (n_blocks - 5), 2))
            k = k_ref[item, pl.ds(key_start(t), K_BLK), :]
            s = lax.dot_general(q[i * Q_BLK:(i + 1) * Q_BLK], k, contract_last, preferred_element_type=F32)
            s_bufs[slot][i * Q_BLK:(i + 1) * Q_BLK, :] = s + bias_ref[geom]

    def softmax(slot):
        s = s_bufs[slot][...]
        sc = sc_bufs[slot][...]
        m = jnp.maximum(jnp.max(s, axis=-1, keepdims=True), jnp.max(sc, axis=-1, keepdims=True))
        p = jnp.exp2(s - m)
        pc = jnp.exp2(sc - m)
        den_bufs[slot][...] = jnp.sum(p, axis=-1, keepdims=True) + jnp.sum(pc, axis=-1, keepdims=True)
        p_bufs[slot][...] = p.astype(BF16)
        pc_bufs[slot][...] = pc.astype(BF16)

    def weighted_sum(g, slot):
        item, local, row0 = locate(g)
        o_ctx = _dot(pc_bufs[slot][...], vc_ref[item])
        outs = []
        for i in range(ATTN_GROUP):
            v = v_ref[item, pl.ds(key_start(local * ATTN_GROUP + i), K_BLK), :]
            outs.append(_dot(p_bufs[slot][i * Q_BLK:(i + 1) * Q_BLK, :], v))
        o = (jnp.concatenate(outs, axis=0) + o_ctx) / den_bufs[slot][...]
        o_ref[item, pl.ds(row0, g_rows), :] = o.astype(o_ref.dtype)

    scores(0, 0)
    scores(1, 1)
    softmax(0)

    def steady(j, carry):
        for parity in range(2):
            tau = 2 * j + 2 + parity
            scores(tau, parity)
            softmax(1 - parity)
            weighted_sum(tau - 2, parity)
        return carry

    lax.fori_loop(0, (n_groups - 2) // 2, steady, 0)
    softmax(1)
    weighted_sum(n_groups - 2, 0)
    weighted_sum(n_groups - 1, 1)


def _neighbourhood_attention(qkv, kvc, strip, cast_srcs=()):
    bsz, n, _ = qkv.shape
    ctx_len = kvc.shape[1]
    rows = n // GRID_W
    n_blocks = rows // Q_ROWS
    assert bsz % ATTN_ITEMS == 0 and n_blocks % ATTN_GROUP == 0
    n_groups = ATTN_ITEMS * (n_blocks // ATTN_GROUP)
    assert n_groups % 2 == 0 and n_groups >= 4
    n_batch_steps = bsz // ATTN_ITEMS
    jobs = [_cast_job(src, layer, N_HEADS * n_batch_steps) for src, layer in cast_srcs]
    cast_in, cast_out, cast_shapes = _cast_io(jobs, lambda h, b: h * n_batch_steps + b)
    layout = _attn_geometries(rows)
    kern = functools.partial(_attn_kernel, n_blocks=n_blocks, last_kb=rows - KEY_ROWS, n_cast=len(jobs), layout=layout)
    seq_spec = lambda off: pl.BlockSpec((ATTN_ITEMS, n, HEAD_DIM), lambda h, b: (b, 0, off + h))
    ctx_spec = lambda off: pl.BlockSpec((ATTN_ITEMS, ctx_len, HEAD_DIM), lambda h, b: (b, 0, off + h))
    g_rows = ATTN_GROUP * Q_BLK
    slot_pair = lambda cols, dtype: [pltpu.VMEM((g_rows, cols), dtype)] * 2
    return pl.pallas_call(
        kern,
        grid=(N_HEADS, n_batch_steps),
        in_specs=[
            seq_spec(0), seq_spec(N_HEADS), seq_spec(2 * N_HEADS),
            ctx_spec(0), ctx_spec(N_HEADS),
            pl.BlockSpec((1,) + strip.shape[1:], lambda h, b: (h, 0, 0, 0)),
        ] + cast_in,
        out_specs=[pl.BlockSpec((ATTN_ITEMS, n, HEAD_DIM), lambda h, b: (b, 0, h))] + cast_out,
        out_shape=[jax.ShapeDtypeStruct((bsz, n, D_MODEL), BF16)] + cast_shapes,
        scratch_shapes=(slot_pair(K_BLK, F32) + slot_pair(ctx_len, F32) + slot_pair(K_BLK, BF16)
                        + slot_pair(ctx_len, BF16) + slot_pair(1, F32)
                        + [pltpu.VMEM((len(layout), Q_BLK, K_BLK), F32)]),
        compiler_params=_params("arbitrary", "arbitrary"),
        name="neighbourhood_attention",
    )(qkv, qkv, qkv, kvc, kvc, strip, *[job.src for job in jobs])


def _proj_res_ln_kernel(y_ref, w_ref, h_ref, g_ref, lg_ref, lb_ref, o_ref):
    sub = y_ref.shape[0] // PROJ_SUB_BLOCKS
    for r in range(PROJ_SUB_BLOCKS):
        rows = slice(r * sub, (r + 1) * sub)
        o_ref[rows, :] = ALPHA * h_ref[rows, :] + g_ref[0] * _dot(y_ref[rows, :], w_ref[...])
        _layer_norm_in_place(o_ref, rows, lg_ref[...], lb_ref[...])


def _proj_res_ln(y, w, h, gate, ln_g, ln_b, rows_per_batch, tm=1024):
    m, k = y.shape
    d = w.shape[1]
    tiles_per_batch = rows_per_batch // tm
    return pl.pallas_call(
        _proj_res_ln_kernel,
        grid=(m // tm,),
        in_specs=[
            pl.BlockSpec((tm, k), lambda i: (i, 0)),
            pl.BlockSpec((k, d), lambda i: (0, 0), pipeline_mode=pl.Buffered(1)),
            pl.BlockSpec((tm, d), lambda i: (i, 0)),
            pl.BlockSpec((1, 1, d), lambda i: (i // tiles_per_batch, 0, 0)),
            pl.BlockSpec((1, d), lambda i: (0, 0)),
            pl.BlockSpec((1, d), lambda i: (0, 0)),
        ],
        out_specs=pl.BlockSpec((tm, d), lambda i: (i, 0)),
        out_shape=jax.ShapeDtypeStruct((m, d), F32),
        compiler_params=_params("parallel"),
        name="proj_res_ln",
    )(y, w, h, gate, ln_g.reshape(1, d), ln_b.reshape(1, d))


def _ffn_kernel(h_ref, sh_ref, sc_ref, g_ref, w_in_hbm, w_out_hbm, lg_ref, lb_ref, o_ref,
                a_ref, wg_buf, wu_buf, wo_buf, sem, *, n_f, tf):
    i = pl.program_id(0)
    n_tiles = pl.num_programs(0)
    d_ff = n_f * tf

    def chunk_copies(f, slot):
        c0 = pl.multiple_of(f * tf, tf)
        return (
            pltpu.make_async_copy(w_in_hbm.at[:, pl.ds(c0, tf)], wg_buf.at[slot], sem.at[slot, 0]),
            pltpu.make_async_copy(w_in_hbm.at[:, pl.ds(d_ff + c0, tf)], wu_buf.at[slot], sem.at[slot, 1]),
            pltpu.make_async_copy(w_out_hbm.at[pl.ds(c0, tf), :], wo_buf.at[slot], sem.at[slot, 2]),
        )

    @pl.when(i == 0)
    def _():
        for copy in chunk_copies(0, 0):
            copy.start()

    a_ref[...] = (h_ref[...] * (1.0 + sc_ref[0]) + sh_ref[0]).astype(BF16)

    def chunk(f, first):
        slot = (i * n_f + f) % 2
        for copy in chunk_copies(f, slot):
            copy.wait()

        @pl.when((f + 1 < n_f) | (i + 1 < n_tiles))
        def _():
            for copy in chunk_copies((f + 1) % n_f, 1 - slot):
                copy.start()

        a = a_ref[...]
        act = _silu(_dot(a, wg_buf[slot])) * _dot(a, wu_buf[slot])
        y = _dot(act.astype(BF16), wo_buf[slot])
        if first:
            o_ref[...] = y
        else:
            o_ref[...] += y

    chunk(0, True)
    lax.fori_loop(1, n_f, lambda f, carry: chunk(f, False), None)
    gate, lg, lb = g_ref[0], lg_ref[...], lb_ref[...]
    for r0 in range(0, o_ref.shape[0], LN_STRIP_ROWS):
        strip = slice(r0, r0 + LN_STRIP_ROWS)
        o_ref[strip, :] = _layer_norm(ALPHA * h_ref[strip, :] + gate * o_ref[strip, :], lg, lb)


def _ffn(h, sh, sc, gate, w_in, w_out, ln_g, ln_b, rows_per_batch, tm=1024, tf=512):
    m, d = h.shape
    d_ff = w_out.shape[0]
    n_f = d_ff // tf
    tiles_per_batch = rows_per_batch // tm
    mod_spec = pl.BlockSpec((1, 1, d), lambda i: (i // tiles_per_batch, 0, 0))
    vec_spec = pl.BlockSpec((1, d), lambda i: (0, 0))
    hbm_spec = pl.BlockSpec(memory_space=pl.ANY)
    return pl.pallas_call(
        functools.partial(_ffn_kernel, n_f=n_f, tf=tf),
        grid=(m // tm,),
        in_specs=[
            pl.BlockSpec((tm, d), lambda i: (i, 0)),
            mod_spec, mod_spec, mod_spec,
            hbm_spec, hbm_spec,
            vec_spec, vec_spec,
        ],
        out_specs=pl.BlockSpec((tm, d), lambda i: (i, 0)),
        out_shape=jax.ShapeDtypeStruct((m, d), F32),
        scratch_shapes=[
            pltpu.VMEM((tm, d), BF16),
            pltpu.VMEM((2, d, tf), BF16), pltpu.VMEM((2, d, tf), BF16), pltpu.VMEM((2, tf, d), BF16),
            pltpu.SemaphoreType.DMA((2, 3)),
        ],
        compiler_params=_params("arbitrary"),
        name="swiglu_ffn",
    )(h, sh, sc, gate, w_in, w_out, ln_g.reshape(1, d), ln_b.reshape(1, d))


def _gmlp_kernel(*refs, n_cast):
    h_ref, sh_ref, sc_ref, wu_ref, wv_ref, lg_ref, lb_ref, ws_ref, bs_ref = refs[:9]
    o_ref = refs[9 + n_cast]
    u_ref, vf_ref, v_ref = refs[10 + 2 * n_cast:]
    _run_casts(refs[9:9 + n_cast], refs[10 + n_cast:10 + 2 * n_cast])

    def gelu(z):
        return 0.5 * z * (1.0 + lax.erf(z * (2.0 ** -0.5)))

    d = u_ref.shape[1]
    col_chunks = [slice(c * GMLP_COL_CHUNK, (c + 1) * GMLP_COL_CHUNK) for c in range(d // GMLP_COL_CHUNK)]
    for r0 in range(0, h_ref.shape[0], GMLP_SUB_ROWS):
        sub = slice(r0, r0 + GMLP_SUB_ROWS)
        a = (h_ref[sub, :] * (1.0 + sc_ref[0]) + sh_ref[0]).astype(BF16)
        for cols in col_chunks:
            vf_ref[sub, cols] = gelu(_dot(a, wv_ref[:, cols]))
        for s0 in range(r0, r0 + GMLP_SUB_ROWS, BF16_SUBLANES):
            strip = slice(s0, s0 + BF16_SUBLANES)
            v_ref[strip, :] = _layer_norm(vf_ref[strip, :], lg_ref[...], lb_ref[...]).astype(BF16)
        for cols in col_chunks:
            u_ref[sub, cols] = gelu(_dot(a, wu_ref[:, cols]))
        for c0 in range(r0, r0 + GMLP_SUB_ROWS, CHUNK):
            rows = slice(c0, c0 + CHUNK)
            for g in range(N_GROUPS):
                cols = slice(g * GROUP_DIM, (g + 1) * GROUP_DIM)
                mixed = _dot(ws_ref[g], v_ref[rows, cols]) + bs_ref[:, cols]
                o_ref[rows, cols] = (u_ref[rows, cols] * mixed).astype(o_ref.dtype)


def _gmlp_gate(h, sh, sc, w_in, ln_g, ln_b, w_s, b_s_full, rows_per_batch, tm=512, cast_srcs=()):
    m, d = h.shape
    tiles_per_batch = rows_per_batch // tm
    mod_spec = pl.BlockSpec((1, 1, d), lambda i: (i // tiles_per_batch, 0, 0))
    vec_spec = pl.BlockSpec((1, d), lambda i: (0, 0))
    jobs = [_cast_job(src, layer, m // tm) for src, layer in cast_srcs]
    cast_in, cast_out, cast_shapes = _cast_io(jobs, lambda i: i)
    return pl.pallas_call(
        functools.partial(_gmlp_kernel, n_cast=len(jobs)),
        grid=(m // tm,),
        in_specs=[
            pl.BlockSpec((tm, d), lambda i: (i, 0)),
            mod_spec, mod_spec,
            pl.BlockSpec((d, d), lambda i: (0, 0)),
            pl.BlockSpec((d, d), lambda i: (0, 1)),
            vec_spec, vec_spec,
            pl.BlockSpec((N_GROUPS, CHUNK, CHUNK), lambda i: (0, 0, 0)),
            pl.BlockSpec((CHUNK, d), lambda i: (0, 0)),
        ] + cast_in,
        out_specs=[pl.BlockSpec((tm, d), lambda i: (i, 0))] + cast_out,
        out_shape=[jax.ShapeDtypeStruct((m, d), BF16)] + cast_shapes,
        scratch_shapes=[pltpu.VMEM((tm, d), F32), pltpu.VMEM((tm, d), F32), pltpu.VMEM((tm, d), BF16)],
        compiler_params=_params("arbitrary"),
        name="gmlp_gate",
    )(h, sh, sc, w_in, w_in, ln_g.reshape(1, d), ln_b.reshape(1, d), w_s, b_s_full, *[job.src for job in jobs])


def kernel(x, c, ctx, c_ctx, ada_w, ada_b, ln_g, ln_b, na_w_qkv, na_w_o, na_rpb, gm_w_in, gm_ln_g, gm_ln_b,
           gm_w_s, gm_b_s, gm_w_out, ffn_w_in, ffn_w_out):
    bsz, n, d = x.shape
    ctx_len = ctx.shape[1]
    rows = n // GRID_W
    assert d == D_MODEL and n % (GRID_W * Q_ROWS) == 0 and rows >= KEY_ROWS + 2 * Q_ROWS

    cond = jnp.concatenate([c, c_ctx[None, :], jnp.zeros((8 - bsz - 1, d), F32)], axis=0)
    ada, w_qkv = _ada_params(cond, ada_w, ada_b, cast_srcs=[(na_w_qkv, 0)])

    def mod_vectors(layer):
        parts = jnp.split(ada[layer], 6, axis=-1)
        latent = [p[:bsz].reshape(bsz, 1, d) for p in parts]
        context = [p[bsz:bsz + 1].reshape(1, 1, d) for p in parts]
        return latent, context

    h = x.reshape(bsz * n, d)

    (sh1, sc1, g1, sh2, sc2, g2), (csh1, csc1, _, _, _, _) = mod_vectors(0)
    q_scale = jnp.concatenate([jnp.full((1, d), HEAD_DIM ** -0.5 * LOG2_E, F32), jnp.ones((1, 2 * d), F32)], axis=1)
    qkv, = _mod_matmul(h, sh1, sc1, w_qkv, q_scale, n, tm=512, tn=6144, name="qkv_proj")
    kvc, = _mod_matmul(ctx.reshape(bsz * ctx_len, d), csh1, csc1, w_qkv, jnp.ones((1, 2 * d), F32), ctx_len,
                       tm=bsz * ctx_len, tn=1024, col_start=d, name="ctx_kv_proj")
    strip = _attn_bias_strip(na_rpb[0])
    o, w_o, gm_w_in_bf, gm_w_out_bf, ffn_w_in0, ffn_w_out0 = _neighbourhood_attention(
        qkv.reshape(bsz, n, 3 * d), kvc.reshape(bsz, ctx_len, 2 * d), strip,
        cast_srcs=[(na_w_o, 0), (gm_w_in, 0), (gm_w_out, 0), (ffn_w_in, 0), (ffn_w_out, 0)])
    h = _proj_res_ln(o.reshape(bsz * n, d), w_o, h, g1, ln_g[0, 0], ln_b[0, 0], n)
    h = _ffn(h, sh2, sc2, g2, ffn_w_in0, ffn_w_out0, ln_g[0, 1], ln_b[0, 1], n)

    (sh1, sc1, g1, sh2, sc2, g2), _ = mod_vectors(1)
    b_s_full = jnp.repeat(jnp.transpose(gm_b_s[0]), GROUP_DIM, axis=1)
    uv, ffn_w_in1, ffn_w_out1 = _gmlp_gate(h, sh1, sc1, gm_w_in_bf, gm_ln_g[0], gm_ln_b[0], gm_w_s[0].astype(BF16),
                                            b_s_full, n, cast_srcs=[(ffn_w_in, 1), (ffn_w_out, 1)])
    h = _proj_res_ln(uv, gm_w_out_bf, h, g1, ln_g[1, 0], ln_b[1, 0], n)
    h = _ffn(h, sh2, sc2, g2, ffn_w_in1, ffn_w_out1, ln_g[1, 1], ln_b[1, 1], n)
    return h.reshape(bsz, n, d)
```
